```python
import math
import jax, jax.numpy as jnp
from jax import lax
import numpy as np

D_MODEL = 2048
BATCH = 4
SEQ = 4096
DEPTH = 1
DEC_BATCH = 32
DEC_SEQ = 4
PAST_LEN = 16384
PAGE_SIZE = 128

f32 = jnp.float32
N_HEADS = 8
HEAD_DIM = 128
D_ATTN = N_HEADS * HEAD_DIM
IDX_HEADS = 8
IDX_DIM = 64
INDEXER_SCALE = (IDX_HEADS ** -0.5) * (IDX_DIM ** -0.5)
TOPK_MAX = 256
Q_BLOCK = 64
NUM_BUCKETS = 32
MAX_DISTANCE = 128
SSM_GROUP = 16
D_SSM = D_MODEL // 2
N_GROUPS = D_SSM // SSM_GROUP
STATE_DIM = 64
SSM_CHUNK = 128
N_EXPERT_GROUPS = 4
EXPERTS_PER_GROUP = 8
N_EXPERTS = N_EXPERT_GROUPS * EXPERTS_PER_GROUP
TOP_K_EXPERT = 2
D_EXPERT = D_MODEL // 8
ALPHA = (2 * DEPTH) ** 0.25
BETA = (8 * DEPTH) ** -0.25
SPLIT_SIZES = (D_ATTN, D_ATTN, D_ATTN, IDX_HEADS * IDX_DIM, IDX_DIM, IDX_HEADS, D_SSM, D_MODEL, D_MODEL)
D_IN = sum(SPLIT_SIZES)

kernel_name = "hybrid_dsa_s5_hmoe_step"


def layer_norm(x, g, b, eps=1e-5):
    xf = x.astype(f32)
    mu = jnp.mean(xf, axis=-1, keepdims=True)
    var = jnp.mean(jnp.square(xf - mu), axis=-1, keepdims=True)
    return ((xf - mu) * lax.rsqrt(var + eps) * g.astype(f32) + b.astype(f32)).astype(x.dtype)


def rel_bucket(dist):
    max_exact = NUM_BUCKETS // 2
    d = jnp.maximum(dist, 0)
    large = max_exact + (jnp.log(jnp.maximum(d, 1).astype(f32) / max_exact)
                         / math.log(MAX_DISTANCE / max_exact) * (NUM_BUCKETS - max_exact)).astype(jnp.int32)
    large = jnp.minimum(large, NUM_BUCKETS - 1)
    return jnp.where(d < max_exact, d, large)


def project_inputs(x, w_in):
    B, T = x.shape[:2]
    z = jnp.einsum('btd,de->bte', x, w_in)
    offsets, acc = [], 0
    for s in SPLIT_SIZES[:-1]:
        acc += s
        offsets.append(acc)
    q, k, v, qi, ki, wi, u, ga, gb = jnp.split(z, offsets, axis=-1)
    heads = lambda a: a.reshape(B, T, N_HEADS, HEAD_DIM)
    return (heads(q), heads(k), heads(v), qi.reshape(B, T, IDX_HEADS, IDX_DIM), ki,
            wi * INDEXER_SCALE, u.reshape(B, T, N_GROUPS, SSM_GROUP), ga, gb)


def indexer_topk(qi, wi, ki, q_pos, k_top):
    s = jnp.einsum('bthd,bsd->bths', qi, ki, preferred_element_type=f32)
    score = jnp.einsum('bth,bths->bts', wi.astype(f32), jax.nn.relu(s))
    causal = jnp.arange(ki.shape[1])[None, :] <= q_pos[:, None]
    score = jnp.where(causal[None], score, -jnp.inf)
    _, idx = lax.top_k(score, k_top)
    valid = idx <= q_pos[None, :, None]
    return idx, valid


def attend_selected(q, k_sel, v_sel, sel_pos, valid, q_pos, rel_bias):
    logits = jnp.einsum('bthd,btkhd->bthk', q, k_sel, preferred_element_type=f32) * (HEAD_DIM ** -0.5)
    bias = rel_bias.astype(f32)[rel_bucket(q_pos[None, :, None] - sel_pos)]
    logits = logits + jnp.moveaxis(bias, -1, 2)
    logits = jnp.where(valid[:, :, None, :], logits, -jnp.inf)
    p = jax.nn.softmax(logits, axis=-1)
    return jnp.einsum('bthk,btkhd->bthd', p, v_sel.astype(f32)).astype(q.dtype)


def take_rows(rows, idx):
    return jax.vmap(lambda r, ix: r[ix])(rows, idx)


def prompt_sparse_attention(q, k, v, qi, wi, ki, rel_bias):
    B, T = q.shape[:2]
    k_top = min(TOPK_MAX, T // 4)
    nb = T // Q_BLOCK

    def block(args):
        i, qb, qib, wib = args
        q_pos = i * Q_BLOCK + jnp.arange(Q_BLOCK)
        idx, valid = indexer_topk(qib, wib, ki, q_pos, k_top)
        return attend_selected(qb, take_rows(k, idx), take_rows(v, idx), idx, valid, q_pos, rel_bias)

    to_blocks = lambda a: jnp.moveaxis(a.reshape(B, nb, Q_BLOCK, *a.shape[2:]), 1, 0)
    out = lax.map(block, (jnp.arange(nb), to_blocks(q), to_blocks(qi), to_blocks(wi)))
    return jnp.moveaxis(out, 0, 1).reshape(B, T, N_HEADS, HEAD_DIM)


def sample_sparse_attention(q, k_new, v_new, qi, wi, ki_new, cache_k, cache_v, cache_kidx, page_table, layer, rel_bias):
    Bd, Tn = q.shape[:2]
    past = page_table.shape[1] * PAGE_SIZE
    k_top = min(TOPK_MAX, (past + Tn) // 4)
    ki_past = cache_kidx[layer, page_table].reshape(Bd, past, IDX_DIM).astype(ki_new.dtype)
    ki_all = jnp.concatenate([ki_past, ki_new], axis=1)
    q_pos = past + jnp.arange(Tn)
    idx, valid = indexer_topk(qi, wi, ki_all, q_pos, k_top)
    is_new = (idx >= past)[..., None, None]
    p_idx = jnp.minimum(idx, past - 1)
    phys = jax.vmap(lambda pt, ix: pt[ix])(page_table, p_idx // PAGE_SIZE)
    slot = p_idx % PAGE_SIZE
    n_idx = jnp.clip(idx - past, 0, Tn - 1)
    k_sel = jnp.where(is_new, take_rows(k_new, n_idx), cache_k[layer, phys, slot].astype(k_new.dtype))
    v_sel = jnp.where(is_new, take_rows(v_new, n_idx), cache_v[layer, phys, slot].astype(v_new.dtype))
    return attend_selected(q, k_sel, v_sel, idx, valid, q_pos, rel_bias)


def ssm_discretize(lam_re, lam_im, log_step, b_re, b_im):
    lam = lax.complex(lam_re.astype(f32), lam_im.astype(f32))
    step = jnp.exp(log_step.astype(f32))[:, None]
    lam_bar = jnp.exp(lam * step)
    b = lax.complex(b_re.astype(f32), b_im.astype(f32))
    b_bar = ((lam_bar - 1.0) / lam)[..., None] * b
    return lam_bar, b_bar


def ssm_block(u, h0, lam_bar, b_bar, c, d):
    uf = u.astype(f32)
    bu = jnp.einsum('gph,btgh->btgp', b_bar, uf)
    a = jnp.broadcast_to(lam_bar, bu.shape)
    combine = lambda l, r: (l[0] * r[0], r[0] * l[1] + r[1])
    a_cum, h = lax.associative_scan(combine, (a, bu), axis=1)
    h = h + a_cum * h0[:, None]
    y = jnp.real(jnp.einsum('ghp,btgp->btgh', c, h)) + d.astype(f32) * uf
    return y, h[:, -1]


def ssm_prompt(u, lam_bar, b_bar, c, d):
    B, T = u.shape[:2]
    nc = T // SSM_CHUNK
    uc = jnp.moveaxis(u.reshape(B, nc, SSM_CHUNK, N_GROUPS, SSM_GROUP), 1, 0)
    h0 = jnp.zeros((B, N_GROUPS, STATE_DIM), jnp.complex64)

    def step(h, u_chunk):
        y, h_next = ssm_block(u_chunk, h, lam_bar, b_bar, c, d)
        return h_next, y

    h_last, ys = lax.scan(step, h0, uc)
    return jnp.moveaxis(ys, 0, 1).reshape(B, T, D_SSM), h_last


def merge_branches(x, attn, ssm_y, ga, gb, w_glu, w_attn_out, w_ssm_out, w_o):
    B, T = x.shape[:2]
    a = attn.reshape(B, T, D_ATTN) @ w_attn_out
    z = jax.nn.gelu(ssm_y.astype(x.dtype))
    s = (z * jax.nn.sigmoid(z @ w_glu)) @ w_ssm_out
    return (jax.nn.sigmoid(ga) * a + jax.nn.sigmoid(gb) * s) @ w_o


def hier_moe(x, w_grp, b_grp, w_exp, b_exp, w_gate, w_up, w_down):
    xt = x.reshape(-1, D_MODEL)
    n = xt.shape[0]
    glog = (xt @ w_grp).astype(f32) + b_grp.astype(f32)
    g_idx = jnp.argmax(glog, axis=-1)
    g_w = jnp.take_along_axis(jax.nn.softmax(glog, axis=-1), g_idx[:, None], axis=-1)
    elog = ((xt @ w_exp).astype(f32) + b_exp.astype(f32)).reshape(n, N_EXPERT_GROUPS, EXPERTS_PER_GROUP)
    elog_g = jnp.take_along_axis(elog, g_idx[:, None, None], axis=1)[:, 0]
    top_v, top_i = lax.top_k(elog_g, TOP_K_EXPERT)
    top_w = jax.nn.softmax(top_v, axis=-1) * g_w
    expert_id = g_idx[:, None] * EXPERTS_PER_GROUP + top_i
    gates = jnp.sum(jax.nn.one_hot(expert_id, N_EXPERTS, dtype=f32) * top_w[..., None], axis=1)

    def expert(acc, args):
        wg, wu, wd, ge = args
        h = jax.nn.silu(xt @ wg) * (xt @ wu)
        return acc + ge[:, None] * (h @ wd).astype(f32), None

    y, _ = lax.scan(expert, jnp.zeros((n, D_MODEL), f32), (w_gate, w_up, w_down, gates.T))
    return y.astype(x.dtype).reshape(x.shape)


def post_norm_block(x, mix, ln1_g, ln1_b, ln2_g, ln2_b, moe_w):
    x1 = layer_norm(ALPHA * x + mix, ln1_g, ln1_b)
    return layer_norm(ALPHA * x1 + hier_moe(x1, *moe_w), ln2_g, ln2_b)


def setup_inputs(seed: int = 0) -> dict:
    key = jax.random.key(seed)
    ks = iter(jax.random.split(key, 40))
    nrm = lambda shape, scale: scale * jax.random.normal(next(ks), shape, f32)
    n_pages = PAST_LEN // PAGE_SIZE
    n_used = DEC_BATCH * n_pages
    n_phys = n_used + max(1, n_used // 4)
    perm = jax.random.permutation(next(ks), n_phys)
    page_table = perm[:n_used].reshape(DEC_BATCH, n_pages).astype(jnp.int32)
    L = DEPTH
    lam_im = jnp.pi * jnp.arange(STATE_DIM, dtype=f32)
    return {
        'x_prompt': nrm((BATCH, SEQ, D_MODEL), 1.0),
        'x_sample': nrm((DEC_BATCH, DEC_SEQ, D_MODEL), 1.0),
        'cache_k': nrm((L, n_phys, PAGE_SIZE, N_HEADS, HEAD_DIM), 1.0),
        'cache_v': nrm((L, n_phys, PAGE_SIZE, N_HEADS, HEAD_DIM), 1.0),
        'cache_kidx': nrm((L, n_phys, PAGE_SIZE, IDX_DIM), 1.0),
        'state_ssm_re': nrm((L, DEC_BATCH, N_GROUPS, STATE_DIM), 0.1),
        'state_ssm_im': nrm((L, DEC_BATCH, N_GROUPS, STATE_DIM), 0.1),
        'page_table': page_table,
        'rel_bias': nrm((NUM_BUCKETS, N_HEADS), 0.5),
        'w_in': nrm((L, D_MODEL, D_IN), D_MODEL ** -0.5),
        'ssm_lambda_re': -0.5 + nrm((L, N_GROUPS, STATE_DIM), 0.01),
        'ssm_lambda_im': lam_im + nrm((L, N_GROUPS, STATE_DIM), 0.01),
        'ssm_log_step': jax.random.uniform(next(ks), (L, N_GROUPS), f32, math.log(1e-3), math.log(1e-1)),
        'ssm_b_re': nrm((L, N_GROUPS, STATE_DIM, SSM_GROUP), (2 * SSM_GROUP) ** -0.5),
        'ssm_b_im': nrm((L, N_GROUPS, STATE_DIM, SSM_GROUP), (2 * SSM_GROUP) ** -0.5),
        'ssm_c_re': nrm((L, N_GROUPS, SSM_GROUP, STATE_DIM), (2 * STATE_DIM) ** -0.5),
        'ssm_c_im': nrm((L, N_GROUPS, SSM_GROUP, STATE_DIM), (2 * STATE_DIM) ** -0.5),
        'ssm_d': nrm((L, N_GROUPS, SSM_GROUP), 1.0),
        'w_glu': nrm((L, D_SSM, D_SSM), D_SSM ** -0.5),
        'w_attn_out': nrm((L, D_ATTN, D_MODEL), BETA * D_ATTN ** -0.5),
        'w_ssm_out': nrm((L, D_SSM, D_MODEL), BETA * D_SSM ** -0.5),
        'w_o': nrm((L, D_MODEL, D_MODEL), BETA * D_MODEL ** -0.5),
        'ln1_g': 1.0 + nrm((L, D_MODEL), 0.02),
        'ln1_b': nrm((L, D_MODEL), 0.02),
        'w_group_router': nrm((L, D_MODEL, N_EXPERT_GROUPS), D_MODEL ** -0.5),
        'b_group_router': nrm((L, N_EXPERT_GROUPS), 0.01),
        'w_expert_router': nrm((L, D_MODEL, N_EXPERTS), D_MODEL ** -0.5),
        'b_expert_router': nrm((L, N_EXPERTS), 0.01),
        'w_exp_gate': nrm((L, N_EXPERTS, D_MODEL, D_EXPERT), D_MODEL ** -0.5),
        'w_exp_up': nrm((L, N_EXPERTS, D_MODEL, D_EXPERT), D_MODEL ** -0.5),
        'w_exp_down': nrm((L, N_EXPERTS, D_EXPERT, D_MODEL), BETA * D_EXPERT ** -0.5),
        'ln2_g': 1.0 + nrm((L, D_MODEL), 0.02),
        'ln2_b': nrm((L, D_MODEL), 0.02),
    }


def reference(x_prompt, x_sample, cache_k, cache_v, cache_kidx, state_ssm_re, state_ssm_im, page_table, rel_bias,
              w_in, ssm_lambda_re, ssm_lambda_im, ssm_log_step, ssm_b_re, ssm_b_im, ssm_c_re, ssm_c_im, ssm_d,
              w_glu, w_attn_out, w_ssm_out, w_o, ln1_g, ln1_b, w_group_router, b_group_router,
              w_expert_router, b_expert_router, w_exp_gate, w_exp_up, w_exp_down, ln2_g, ln2_b):
    xp, xs = x_prompt, x_sample
    B, T = xp.shape[:2]
    Bd, Tn = xs.shape[:2]
    sdt = state_ssm_re.dtype
    kp, vp, kip, srp, sip, ksm, vsm, kism, srs, sis = ([] for _ in range(10))
    for l in range(DEPTH):
        lam_bar, b_bar = ssm_discretize(ssm_lambda_re[l], ssm_lambda_im[l], ssm_log_step[l], ssm_b_re[l], ssm_b_im[l])
        c = lax.complex(ssm_c_re[l].astype(f32), ssm_c_im[l].astype(f32))
        moe_w = (w_group_router[l], b_group_router[l], w_expert_router[l], b_expert_router[l],
                 w_exp_gate[l], w_exp_up[l], w_exp_down[l])
        q, k, v, qi, ki, wi, u, ga, gb = project_inputs(xp, w_in[l])
        attn = prompt_sparse_attention(q, k, v, qi, wi, ki, rel_bias)
        ssm_y, h_p = ssm_prompt(u, lam_bar, b_bar, c, ssm_d[l])
        mix = merge_branches(xp, attn, ssm_y, ga, gb, w_glu[l], w_attn_out[l], w_ssm_out[l], w_o[l])
        kp.append(k.reshape(B, T // PAGE_SIZE, PAGE_SIZE, N_HEADS, HEAD_DIM))
        vp.append(v.reshape(B, T // PAGE_SIZE, PAGE_SIZE, N_HEADS, HEAD_DIM))
        kip.append(ki.reshape(B, T // PAGE_SIZE, PAGE_SIZE, IDX_DIM))
        srp.append(jnp.real(h_p).astype(sdt))
        sip.append(jnp.imag(h_p).astype(sdt))
        xp = post_norm_block(xp, mix, ln1_g[l], ln1_b[l], ln2_g[l], ln2_b[l], moe_w)
        q, k, v, qi, ki, wi, u, ga, gb = project_inputs(xs, w_in[l])
        attn = sample_sparse_attention(q, k, v, qi, wi, ki, cache_k, cache_v, cache_kidx, page_table, l, rel_bias)
        h0 = lax.complex(state_ssm_re[l].astype(f32), state_ssm_im[l].astype(f32))
        ssm_y, h_s = ssm_block(u, h0, lam_bar, b_bar, c, ssm_d[l])
        mix = merge_branches(xs, attn, ssm_y.reshape(Bd, Tn, D_SSM), ga, gb, w_glu[l], w_attn_out[l], w_ssm_out[l], w_o[l])
        ksm.append(k)
        vsm.append(v)
        kism.append(ki)
        srs.append(jnp.real(h_s).astype(sdt))
        sis.append(jnp.imag(h_s).astype(sdt))
        xs = post_norm_block(xs, mix, ln1_g[l], ln1_b[l], ln2_g[l], ln2_b[l], moe_w)
    return (xp, xs,
            jnp.stack(kp), jnp.stack(vp), jnp.stack(kip), jnp.stack(srp), jnp.stack(sip),
            jnp.stack(ksm), jnp.stack(vsm), jnp.stack(kism), jnp.stack(srs), jnp.stack(sis))
```

```python
import functools
import math

import numpy as np
import jax
import jax.numpy as jnp
from jax import lax
from jax.experimental import pallas as pl
from jax.experimental.pallas import tpu as pltpu

f32, bf16, i32 = jnp.float32, jnp.bfloat16, jnp.int32

N_HEADS = 8
HEAD_DIM = 128
IDX_HEADS = 8
IDX_DIM = 64
TOPK_MAX = 256
NUM_BUCKETS = 32
MAX_DISTANCE = 128
SSM_GROUP = 16
STATE_DIM = 64
N_EXPERT_GROUPS = 4
EXPERTS_PER_GROUP = 8
TOP_K_EXPERT = 2
DEPTH = 1
ALPHA = (2 * DEPTH) ** 0.25
LN_EPS = 1e-5
INDEXER_SCALE = (IDX_HEADS ** -0.5) * (IDX_DIM ** -0.5)

LANES = 128
SUBLANES = 8
VMEM_LIMIT = 56 * 1024 * 1024

INT_MIN = np.int32(-2 ** 31)
KEY_NEG_INF = np.int32(-2139095041)
NEG_BIG = -1e30


def _cparams(sem):
    return pltpu.CompilerParams(dimension_semantics=sem, vmem_limit_bytes=VMEM_LIMIT)


def _round_up(a, b):
    return (a + b - 1) // b * b


def _mm_nn_kernel(a_ref, b_ref, o_ref):
    o_ref[...] = jnp.dot(a_ref[...], b_ref[...], preferred_element_type=f32).astype(o_ref.dtype)


def _mm_nn(a, b, tm, tn, out_dtype=f32):
    M, K = a.shape
    N = b.shape[1]
    return pl.pallas_call(
        _mm_nn_kernel, grid=(M // tm, N // tn),
        in_specs=[pl.BlockSpec((tm, K), lambda i, j: (i, 0)), pl.BlockSpec((K, tn), lambda i, j: (0, j))],
        out_specs=pl.BlockSpec((tm, tn), lambda i, j: (i, j)),
        out_shape=jax.ShapeDtypeStruct((M, N), out_dtype),
        compiler_params=_cparams(("parallel", "arbitrary")), name="proj_nn")(a, b)


def _mm_nt_kernel(w_ref, x_ref, o_ref):
    o_ref[...] = lax.dot_general(w_ref[...], x_ref[...], (((1,), (1,)), ((), ())),
                                 preferred_element_type=f32).astype(o_ref.dtype)


def _mm_nt(w, x, tn, tm, out_dtype):
    n, K = w.shape
    M = x.shape[0]
    return pl.pallas_call(
        _mm_nt_kernel, grid=(M // tm, n // tn),
        in_specs=[pl.BlockSpec((tn, K), lambda i, j: (j, 0)), pl.BlockSpec((tm, K), lambda i, j: (i, 0))],
        out_specs=pl.BlockSpec((tn, tm), lambda i, j: (j, i)),
        out_shape=jax.ShapeDtypeStruct((n, M), out_dtype),
        compiler_params=_cparams(("parallel", "arbitrary")), name="proj_nt")(w, x)


def _order_key(x):
    b = lax.bitcast_convert_type(x + 0.0, i32)
    return b ^ ((b >> 31) & np.int32(0x7FFFFFFF))


def _rel_bucket(dist):
    max_exact = NUM_BUCKETS // 2
    d = jnp.maximum(dist, 0)
    large = max_exact + (jnp.log(jnp.maximum(d, 1).astype(f32) / max_exact)
                         / math.log(MAX_DISTANCE / max_exact) * (NUM_BUCKETS - max_exact)).astype(i32)
    large = jnp.minimum(large, NUM_BUCKETS - 1)
    return jnp.where(d < max_exact, d, large)


def _select_rows(keys_sc, n_rows, k_top, idx_bits, slab):
    tq = keys_sc.shape[1]
    n_slabs = n_rows // slab

    def count(pred):
        def body(i, cnt):
            r0 = pl.multiple_of(i * slab, slab)
            blk = keys_sc[pl.ds(r0, slab), :]
            hit = jnp.where(pred(blk, r0), 1, 0).astype(i32)
            return cnt + jnp.sum(hit.reshape(slab // SUBLANES, SUBLANES, tq), axis=0)
        cnt = lax.fori_loop(0, n_slabs, body, jnp.zeros((SUBLANES, tq), i32))
        return jnp.sum(cnt, axis=0, keepdims=True)

    def count_ge(cand):
        cb = jnp.broadcast_to(cand, (slab, tq))
        return count(lambda blk, r0: blk >= cb)

    zero = jnp.zeros((1, tq), i32)
    thr = jnp.where(count_ge(zero) >= k_top, zero, jnp.full((1, tq), INT_MIN, i32))

    def bit_body(it, thr):
        cand = thr | lax.shift_left(np.int32(1), np.int32(30) - it)
        return jnp.where(count_ge(cand) >= k_top, cand, thr)

    thr = lax.fori_loop(0, 31, bit_body, thr)
    thr = jnp.maximum(thr, KEY_NEG_INF)
    n_ge = count_ge(thr)

    @pl.when(jnp.max(n_ge) > k_top)
    def _():
        need = k_top - count_ge(thr + 1)
        tb = jnp.broadcast_to(thr, (slab, tq))

        def count_eq_below(m):
            mb = jnp.broadcast_to(m, (slab, tq))
            return count(lambda blk, r0: (blk == tb) & (r0 + lax.broadcasted_iota(i32, (slab, tq), 0) < mb))

        def idx_body(it, m):
            cand = m | lax.shift_left(np.int32(1), np.int32(idx_bits - 1) - it)
            return jnp.where(count_eq_below(cand) < need, cand, m)

        m = lax.fori_loop(0, idx_bits, idx_body, zero)
        mb = jnp.broadcast_to(m, (slab, tq))

        def demote(i, _):
            r0 = pl.multiple_of(i * slab, slab)
            blk = keys_sc[pl.ds(r0, slab), :]
            drop = (blk == tb) & (r0 + lax.broadcasted_iota(i32, (slab, tq), 0) > mb)
            keys_sc[pl.ds(r0, slab), :] = jnp.where(drop, blk - 1, blk)
            return 0

        lax.fori_loop(0, n_slabs, demote, 0)

    return thr


def _pattn_kernel(qiT_ref, wiT_ref, ki_ref, qT_ref, k_ref, vT_ref, bias_ref, oT_ref,
                  keys_sc, thr_sc, m_sc, l_sc, acc_sc, *, k_top, tq, idx_bits):
    qb = pl.program_id(1)
    kb = pl.program_id(2)
    nk = pl.num_programs(2)

    @pl.when(kb == 0)
    def _():
        def chunk(c, _):
            r0 = pl.multiple_of(c * tq, tq)
            ki_c = ki_ref[0, pl.ds(r0, tq), :]
            score = jnp.zeros((tq, tq), f32)
            for h in range(IDX_HEADS):
                s = jnp.dot(ki_c, qiT_ref[h * IDX_DIM:(h + 1) * IDX_DIM, :], preferred_element_type=f32)
                score = score + wiT_ref[h:h + 1, :] * jnp.maximum(s, 0.0)
            kpos = r0 + lax.broadcasted_iota(i32, (tq, tq), 0)
            qpos = qb * tq + lax.broadcasted_iota(i32, (tq, tq), 1)
            keys_sc[pl.ds(r0, tq), :] = jnp.where(kpos <= qpos, _order_key(score), INT_MIN)
            return 0

        lax.fori_loop(0, qb + 1, chunk, 0)
        thr_sc[...] = _select_rows(keys_sc, (qb + 1) * tq, k_top, idx_bits, 64)
        m_sc[...] = jnp.full(m_sc.shape, NEG_BIG, f32)
        l_sc[...] = jnp.zeros(l_sc.shape, f32)
        acc_sc[...] = jnp.zeros(acc_sc.shape, f32)

    @pl.when(kb <= qb)
    def _():
        r0 = pl.multiple_of(kb * tq, tq)
        sel = keys_sc[pl.ds(r0, tq), :] >= thr_sc[...]
        for h in range(N_HEADS):
            hs = slice(h * HEAD_DIM, (h + 1) * HEAD_DIM)
            s = jnp.dot(k_ref[:, hs], qT_ref[hs, :], preferred_element_type=f32) + bias_ref[h, 0]
            s = jnp.where(sel, s, NEG_BIG)
            m_old = m_sc[h:h + 1, :]
            m_new = jnp.maximum(m_old, jnp.max(s, axis=0, keepdims=True))
            p = jnp.where(sel, jnp.exp(s - m_new), 0.0)
            alpha = jnp.exp(m_old - m_new)
            l_sc[h:h + 1, :] = alpha * l_sc[h:h + 1, :] + jnp.sum(p, axis=0, keepdims=True)
            acc_sc[hs, :] = alpha * acc_sc[hs, :] + jnp.dot(vT_ref[hs, :], p.astype(bf16),
                                                            preferred_element_type=f32)
            m_sc[h:h + 1, :] = m_new

    @pl.when(kb == nk - 1)
    def _():
        for h in range(N_HEADS):
            hs = slice(h * HEAD_DIM, (h + 1) * HEAD_DIM)
            oT_ref[hs, :] = acc_sc[hs, :] / l_sc[h:h + 1, :]


def _prompt_attention(qT, vT, qiT, wiT, ki, k, bias_tiles, B, T, tq):
    nq = T // tq
    k_top = min(TOPK_MAX, T // 4)
    d_attn = N_HEADS * HEAD_DIM
    kern = functools.partial(_pattn_kernel, k_top=k_top, tq=tq, idx_bits=max(1, (T - 1).bit_length()))
    col = lambda b, q, kb: (0, b * nq + q)
    kcl = lambda b, q, kb: jnp.minimum(kb, q)
    return pl.pallas_call(
        kern, grid=(B, nq, nq),
        in_specs=[
            pl.BlockSpec((IDX_HEADS * IDX_DIM, tq), col),
            pl.BlockSpec((IDX_HEADS, tq), col),
            pl.BlockSpec((1, T, IDX_DIM), lambda b, q, kb: (b, 0, 0)),
            pl.BlockSpec((d_attn, tq), col),
            pl.BlockSpec((tq, d_attn), lambda b, q, kb: (b * nq + kcl(b, q, kb), 0)),
            pl.BlockSpec((d_attn, tq), lambda b, q, kb: (0, b * nq + kcl(b, q, kb))),
            pl.BlockSpec((N_HEADS, 1, tq, tq), lambda b, q, kb: (0, jnp.clip(q - kb, 0, 2), 0, 0)),
        ],
        out_specs=pl.BlockSpec((d_attn, tq), col),
        out_shape=jax.ShapeDtypeStruct((d_attn, B * T), f32),
        scratch_shapes=[pltpu.VMEM((T, tq), i32), pltpu.VMEM((1, tq), i32),
                        pltpu.VMEM((N_HEADS, tq), f32), pltpu.VMEM((N_HEADS, tq), f32),
                        pltpu.VMEM((d_attn, tq), f32)],
        compiler_params=_cparams(("parallel", "arbitrary", "arbitrary")), name="prompt_attn",
    )(qiT, wiT, ki, qT, k, vT, bias_tiles)


def _prompt_bias_tiles(rel_bias, tq):
    s = jnp.arange(tq)[:, None]
    t = jnp.arange(tq)[None, :]
    tiles = [rel_bias.astype(f32)[_rel_bucket(d * tq + t - s)] for d in range(3)]
    return jnp.transpose(jnp.stack(tiles), (3, 0, 1, 2))


def _ssm_tables(lam_re, lam_im, log_step, b_re, b_im, c_re, c_im, d, L):
    hp = lax.Precision.HIGHEST
    G, P = lam_re.shape
    H = b_re.shape[-1]
    lr, li = lam_re.astype(f32), lam_im.astype(f32)
    step = jnp.exp(log_step.astype(f32))[:, None]
    taus = jnp.arange(L + 1, dtype=f32)[:, None, None]
    mag = jnp.exp(lr * step * taus)
    pr, pi = mag * jnp.cos(li * step * taus), mag * jnp.sin(li * step * taus)
    x, y, den = pr[1] - 1.0, pi[1], lr * lr + li * li
    fr, fi = (x * lr + y * li) / den, (y * lr - x * li) / den
    bbr = fr[..., None] * b_re - fi[..., None] * b_im
    bbi = fr[..., None] * b_im + fi[..., None] * b_re
    clr = c_re[None] * pr[:, :, None, :] - c_im[None] * pi[:, :, None, :]
    cli = c_re[None] * pi[:, :, None, :] + c_im[None] * pr[:, :, None, :]
    kern = (jnp.einsum('tghp,gpk->tghk', clr[:L], bbr, precision=hp)
            - jnp.einsum('tghp,gpk->tghk', cli[:L], bbi, precision=hp))
    tt = jnp.arange(L)[:, None] - jnp.arange(L)[None, :]
    toe = jnp.where((tt >= 0)[:, :, None, None, None], kern[jnp.maximum(tt, 0)], 0.0)
    toe = toe + (jnp.eye(L)[:, :, None, None, None] * jnp.eye(H)[None, None, None] * d.astype(f32)[None, None, :, :, None])
    mT = jnp.transpose(toe, (2, 1, 4, 0, 3)).reshape(G, L * H, L * H)
    rev = jnp.arange(L - 1, -1, -1)
    wr = pr[rev][:, :, :, None] * bbr[None] - pi[rev][:, :, :, None] * bbi[None]
    wi = pr[rev][:, :, :, None] * bbi[None] + pi[rev][:, :, :, None] * bbr[None]
    wr = jnp.transpose(wr, (1, 0, 3, 2)).reshape(G, L * H, P)
    wi = jnp.transpose(wi, (1, 0, 3, 2)).reshape(G, L * H, P)
    vr = jnp.transpose(clr[1:], (1, 3, 0, 2)).reshape(G, P, L * H)
    vi = -jnp.transpose(cli[1:], (1, 3, 0, 2)).reshape(G, P, L * H)
    GP, LH = G // 2, L * H
    z = jnp.zeros((GP, LH, LH), f32)
    mT2 = jnp.concatenate([jnp.concatenate([mT[0::2], z], axis=2), jnp.concatenate([z, mT[1::2]], axis=2)], axis=1)
    zw = jnp.zeros((GP, LH, P), f32)
    w2 = jnp.concatenate([jnp.concatenate([wr[0::2], zw, wi[0::2], zw], axis=2),
                          jnp.concatenate([zw, wr[1::2], zw, wi[1::2]], axis=2)], axis=1)
    zv = jnp.zeros((GP, P, LH), f32)
    v2 = jnp.concatenate([jnp.concatenate([vr[0::2], zv], axis=2), jnp.concatenate([zv, vr[1::2]], axis=2),
                          jnp.concatenate([vi[0::2], zv], axis=2), jnp.concatenate([zv, vi[1::2]], axis=2)], axis=1)
    lamr = jnp.concatenate([pr[L][0::2], pr[L][1::2]], axis=1)[:, None, :]
    lami = jnp.concatenate([pi[L][0::2], pi[L][1::2]], axis=1)[:, None, :]
    return mT2.astype(bf16), w2.astype(bf16), v2.astype(bf16), lamr, lami


def _ssm_kernel(u_ref, mt_ref, w_ref, v_ref, lr_ref, li_ref, h0_ref, y_ref, hout_ref, s_sc, hin_sc, *, cpb, nbp):
    half = LANES
    u = u_ref[...].astype(bf16)
    s_sc[...] = jnp.dot(u, w_ref[0], preferred_element_type=f32).reshape(cpb, nbp, 2 * half)
    lr = jnp.broadcast_to(lr_ref[0], (nbp, half))
    li = jnp.broadcast_to(li_ref[0], (nbp, half))

    def body(c, carry):
        hr, hi = carry
        hin_sc[c, :, 0:half] = hr
        hin_sc[c, :, half:2 * half] = hi
        s = s_sc[c]
        return lr * hr - li * hi + s[:, 0:half], lr * hi + li * hr + s[:, half:2 * half]

    hr, hi = lax.fori_loop(0, cpb, body, (h0_ref[0, :, 0:half], h0_ref[0, :, half:2 * half]))
    hout_ref[0, :, 0:half] = hr
    hout_ref[0, :, half:2 * half] = hi
    hin = hin_sc[...].reshape(cpb * nbp, 2 * half).astype(bf16)
    y_ref[...] = (jnp.dot(u, mt_ref[0], preferred_element_type=f32)
                  + jnp.dot(hin, v_ref[0], preferred_element_type=f32))


def _ssm(u, h0_re, h0_im, tables, L):
    mT2, w2, v2, lamr, lami = tables
    B, T, D = u.shape
    H, P = SSM_GROUP, STATE_DIM
    G = D // H
    GP = G // 2
    cpb = T // L
    nbp = _round_up(B, SUBLANES)
    blk = 2 * L * H
    ur = jnp.transpose(u.reshape(B, cpb, L, G, H), (1, 0, 3, 2, 4))
    ur = jnp.pad(ur, ((0, 0), (0, nbp - B), (0, 0), (0, 0), (0, 0))).reshape(cpb * nbp, G * L * H)

    def pack(a):
        a = jnp.pad(a.astype(f32), ((0, nbp - B), (0, 0), (0, 0)))
        return jnp.transpose(a.reshape(nbp, GP, 2 * P), (1, 0, 2))

    h0 = jnp.concatenate([pack(h0_re), pack(h0_im)], axis=2)
    kern = functools.partial(_ssm_kernel, cpb=cpb, nbp=nbp)
    y, hout = pl.pallas_call(
        kern, grid=(GP,),
        in_specs=[
            pl.BlockSpec((cpb * nbp, blk), lambda g: (0, g)),
            pl.BlockSpec((1, blk, blk), lambda g: (g, 0, 0)),
            pl.BlockSpec((1, blk, 4 * P), lambda g: (g, 0, 0)),
            pl.BlockSpec((1, 4 * P, blk), lambda g: (g, 0, 0)),
            pl.BlockSpec((1, 1, 2 * P), lambda g: (g, 0, 0)),
            pl.BlockSpec((1, 1, 2 * P), lambda g: (g, 0, 0)),
            pl.BlockSpec((1, nbp, 4 * P), lambda g: (g, 0, 0)),
        ],
        out_specs=[pl.BlockSpec((cpb * nbp, blk), lambda g: (0, g)),
                   pl.BlockSpec((1, nbp, 4 * P), lambda g: (g, 0, 0))],
        out_shape=[jax.ShapeDtypeStruct((cpb * nbp, G * L * H), f32),
                   jax.ShapeDtypeStruct((GP, nbp, 4 * P), f32)],
        scratch_shapes=[pltpu.VMEM((cpb, nbp, 4 * P), f32), pltpu.VMEM((cpb, nbp, 4 * P), f32)],
        compiler_params=_cparams(("parallel",)), name="ssm",
    )(ur, mT2, w2, v2, lamr, lami, h0)
    y = jnp.transpose(y.reshape(cpb, nbp, G, L, H)[:, :B], (1, 0, 3, 2, 4)).reshape(B, T, D)
    unpack = lambda a: jnp.transpose(a, (1, 0, 2)).reshape(nbp, G, P)[:B]
    return y, unpack(hout[:, :, :2 * P]), unpack(hout[:, :, 2 * P:])


def _sigmoid(x):
    return 1.0 / (1.0 + jnp.exp(-x))


def _gelu_tanh(x):
    return 0.5 * x * (1.0 + jnp.tanh(math.sqrt(2.0 / math.pi) * (x + 0.044715 * (x * x * x))))


def _layer_norm(r, g, b):
    mu = jnp.mean(r, axis=-1, keepdims=True)
    c = r - mu
    var = jnp.mean(c * c, axis=-1, keepdims=True)
    return c * lax.rsqrt(var + LN_EPS) * g + b


def _merge_kernel(x_ref, attn_ref, y_ref, ga_ref, gb_ref, wa_ref, wg_ref, ws_ref, wo_ref, g_ref, b_ref, o_ref):
    a = jnp.dot(attn_ref[...].astype(bf16), wa_ref[...], preferred_element_type=f32)
    z = _gelu_tanh(y_ref[...])
    gl = jnp.dot(z.astype(bf16), wg_ref[...], preferred_element_type=f32)
    s = jnp.dot((z * _sigmoid(gl)).astype(bf16), ws_ref[...], preferred_element_type=f32)
    m = _sigmoid(ga_ref[...]) * a + _sigmoid(gb_ref[...]) * s
    mix = jnp.dot(m.astype(bf16), wo_ref[...], preferred_element_type=f32)
    o_ref[...] = _layer_norm(ALPHA * x_ref[...] + mix, g_ref[...], b_ref[...])


def _merge(x, attn, ssm_y, zn, ga_blk, gb_blk, wa, wg, ws, wo, g, b, tm):
    N, D = x.shape
    Dh = attn.shape[1]
    const = lambda shape: pl.BlockSpec(shape, lambda i: (0, 0), pipeline_mode=pl.Buffered(1))
    return pl.pallas_call(
        _merge_kernel, grid=(N // tm,),
        in_specs=[pl.BlockSpec((tm, D), lambda i: (i, 0)),
                  pl.BlockSpec((tm, Dh), lambda i: (i, 0)),
                  pl.BlockSpec((tm, Dh), lambda i: (i, 0)),
                  pl.BlockSpec((tm, D), lambda i: (i, ga_blk)),
                  pl.BlockSpec((tm, D), lambda i: (i, gb_blk)),
                  const((Dh, D)), const((Dh, Dh)), const((Dh, D)), const((D, D)),
                  const((1, D)), const((1, D))],
        out_specs=pl.BlockSpec((tm, D), lambda i: (i, 0)),
        out_shape=jax.ShapeDtypeStruct((N, D), f32),
        compiler_params=_cparams(("parallel",)), name="merge_ln1",
    )(x, attn, ssm_y, zn, zn, wa, wg, ws, wo, g, b)


def _router_kernel(x_ref, w_ref, b_ref, o_ref):
    E = N_EXPERT_GROUPS * EXPERTS_PER_GROUP
    logit = jnp.dot(x_ref[...], w_ref[...], preferred_element_type=f32, precision=lax.Precision.HIGHEST) + b_ref[...]
    lane = lax.broadcasted_iota(i32, logit.shape, 1)
    is_g = (lane >= E) & (lane < E + N_EXPERT_GROUPS)
    glog = jnp.where(is_g, logit, -jnp.inf)
    g_max = jnp.max(glog, axis=1, keepdims=True)
    g_idx = jnp.min(jnp.where(glog == g_max, lane, 4 * LANES), axis=1, keepdims=True) - E
    g_w = 1.0 / jnp.sum(jnp.where(is_g, jnp.exp(glog - g_max), 0.0), axis=1, keepdims=True)
    in_g = (lane < E) & (lane // EXPERTS_PER_GROUP == g_idx)
    e1 = jnp.where(in_g, logit, -jnp.inf)
    v1 = jnp.max(e1, axis=1, keepdims=True)
    i1 = jnp.min(jnp.where(e1 == v1, lane, 4 * LANES), axis=1, keepdims=True)
    e2 = jnp.where(lane == i1, -jnp.inf, e1)
    v2 = jnp.max(e2, axis=1, keepdims=True)
    i2 = jnp.min(jnp.where(e2 == v2, lane, 4 * LANES), axis=1, keepdims=True)
    t = jnp.exp(v2 - v1)
    w1 = g_w / (1.0 + t)
    w2 = g_w * t / (1.0 + t)
    o_ref[...] = jnp.where(lane == i1, w1, 0.0) + jnp.where(lane == i2, w2, 0.0)


def _router(x1, w_r, b_r, tm):
    N, D = x1.shape
    return pl.pallas_call(
        _router_kernel, grid=(N // tm,),
        in_specs=[pl.BlockSpec((tm, D), lambda i: (i, 0)),
                  pl.BlockSpec((D, LANES), lambda i: (0, 0)),
                  pl.BlockSpec((1, LANES), lambda i: (0, 0))],
        out_specs=pl.BlockSpec((tm, LANES), lambda i: (i, 0)),
        out_shape=jax.ShapeDtypeStruct((N, LANES), f32),
        compiler_params=_cparams(("parallel",)), name="router",
    )(x1, w_r, b_r)


def _moe_kernel(x_ref, gate_ref, wg_ref, wu_ref, wd_ref, g_ref, b_ref, o_ref, xb_sc, acc_sc):
    e = pl.program_id(1)

    @pl.when(e == 0)
    def _():
        xb_sc[...] = x_ref[...].astype(bf16)
        acc_sc[...] = jnp.zeros(acc_sc.shape, f32)

    gates = gate_ref[...]
    lane = lax.broadcasted_iota(i32, gates.shape, 1)
    ge = jnp.sum(jnp.where(lane == e, gates, 0.0), axis=1, keepdims=True)
    xb = xb_sc[...]
    hg = jnp.dot(xb, wg_ref[0], preferred_element_type=f32)
    hu = jnp.dot(xb, wu_ref[0], preferred_element_type=f32)
    h = hg * _sigmoid(hg) * hu
    acc_sc[...] += jnp.dot((h * ge).astype(bf16), wd_ref[0], preferred_element_type=f32)

    @pl.when(e == pl.num_programs(1) - 1)
    def _():
        o_ref[...] = _layer_norm(ALPHA * x_ref[...] + acc_sc[...], g_ref[...], b_ref[...])


def _moe(x1, gates, wg, wu, wd, g, b, tm):
    N, D = x1.shape
    E, _, De = wg.shape
    return pl.pallas_call(
        _moe_kernel, grid=(N // tm, E),
        in_specs=[pl.BlockSpec((tm, D), lambda i, e: (i, 0)),
                  pl.BlockSpec((tm, LANES), lambda i, e: (i, 0)),
                  pl.BlockSpec((1, D, De), lambda i, e: (e, 0, 0)),
                  pl.BlockSpec((1, D, De), lambda i, e: (e, 0, 0)),
                  pl.BlockSpec((1, De, D), lambda i, e: (e, 0, 0)),
                  pl.BlockSpec((1, D), lambda i, e: (0, 0)),
                  pl.BlockSpec((1, D), lambda i, e: (0, 0))],
        out_specs=pl.BlockSpec((tm, D), lambda i, e: (i, 0)),
        out_shape=jax.ShapeDtypeStruct((N, D), f32),
        scratch_shapes=[pltpu.VMEM((tm, D), bf16), pltpu.VMEM((tm, D), f32)],
        compiler_params=_cparams(("parallel", "arbitrary")), name="moe_ln2",
    )(x1, gates, wg, wu, wd, g, b)


def _sscore_kernel(pt_ref, qi_ref, wi_ref, kn_ref, *rest, pp):
    pages, (o_ref, on_ref) = rest[:pp], rest[pp:]
    qi = qi_ref[0]
    wi = wi_ref[0]
    tn = qi.shape[0] // IDX_HEADS

    def scores(keys):
        s = lax.dot_general(qi, keys, (((1,), (1,)), ((), ())), preferred_element_type=f32)
        r = jnp.maximum(s, 0.0) * wi
        return jnp.sum(r.reshape(tn, IDX_HEADS, keys.shape[0]), axis=1)

    for j in range(pp):
        o_ref[0, :, j * LANES:(j + 1) * LANES] = scores(pages[j][0].astype(bf16))

    @pl.when(pl.program_id(1) == 0)
    def _():
        on_ref[0] = scores(kn_ref[0])


def _sample_scores(page_table, qi, wi, ki_new, cache_kidx, pp):
    Bd, Tn = qi.shape[:2]
    n_pages = page_table.shape[1]
    ps = cache_kidx.shape[1]
    rows = Tn * IDX_HEADS
    qi2 = qi.reshape(Bd, rows, IDX_DIM).astype(bf16)
    wi2 = wi.reshape(Bd, rows, 1)
    kn = jnp.pad(ki_new, ((0, 0), (0, LANES - Tn), (0, 0))).astype(bf16)
    page_spec = lambda j: pl.BlockSpec((1, ps, IDX_DIM), lambda b, p, pt: (pt[b, p * pp + j], 0, 0))
    grid_spec = pltpu.PrefetchScalarGridSpec(
        num_scalar_prefetch=1, grid=(Bd, n_pages // pp),
        in_specs=[pl.BlockSpec((1, rows, IDX_DIM), lambda b, p, pt: (b, 0, 0)),
                  pl.BlockSpec((1, rows, 1), lambda b, p, pt: (b, 0, 0)),
                  pl.BlockSpec((1, LANES, IDX_DIM), lambda b, p, pt: (b, 0, 0))]
                 + [page_spec(j) for j in range(pp)],
        out_specs=[pl.BlockSpec((1, Tn, pp * ps), lambda b, p, pt: (b, 0, p)),
                   pl.BlockSpec((1, Tn, LANES), lambda b, p, pt: (b, 0, 0))])
    return pl.pallas_call(
        functools.partial(_sscore_kernel, pp=pp), grid_spec=grid_spec,
        out_shape=[jax.ShapeDtypeStruct((Bd, Tn, n_pages * ps), f32),
                   jax.ShapeDtypeStruct((Bd, Tn, LANES), f32)],
        compiler_params=_cparams(("parallel", "arbitrary")), name="sample_scores",
    )(page_table, qi2, wi2, kn, *([cache_kidx] * pp))


def _sselect_kernel(sp_ref, sn_ref, o_ref, keys_sc, *, k_top, tn, idx_bits):
    rows = sp_ref.shape[0]
    past = sp_ref.shape[1]
    n_tiles = past // LANES

    def load(j, _):
        c0 = pl.multiple_of(j * LANES, LANES)
        keys_sc[pl.ds(c0, LANES), :] = _order_key(jnp.transpose(sp_ref[:, pl.ds(c0, LANES)]))
        return 0

    lax.fori_loop(0, n_tiles, load, 0)
    new = _order_key(jnp.transpose(sn_ref[...]))
    jj = lax.broadcasted_iota(i32, new.shape, 0)
    qq = lax.broadcasted_iota(i32, new.shape, 1) % tn
    keys_sc[pl.ds(past, LANES), :] = jnp.where(jj <= qq, new, INT_MIN)
    thr = _select_rows(keys_sc, past + LANES, k_top, idx_bits, 64)

    def store(j, _):
        c0 = pl.multiple_of(j * LANES, LANES)
        o_ref[pl.ds(c0, LANES), :] = jnp.where(keys_sc[pl.ds(c0, LANES), :] >= thr, 1.0, 0.0)
        return 0

    lax.fori_loop(0, n_tiles + 1, store, 0)


def _sample_select(scores_past, scores_new, tn):
    assert scores_past.shape[0] <= LANES
    pad_rows = lambda a: jnp.pad(a, ((0, LANES - a.shape[0]), (0, 0)))
    scores_past, scores_new = pad_rows(scores_past), pad_rows(scores_new)
    R, past = scores_past.shape
    k_top = min(TOPK_MAX, (past + tn) // 4)
    kern = functools.partial(_sselect_kernel, k_top=k_top, tn=tn, idx_bits=(past + LANES - 1).bit_length())
    return pl.pallas_call(
        kern, grid=(1,),
        in_specs=[pl.BlockSpec((R, past), lambda i: (0, 0)), pl.BlockSpec((R, LANES), lambda i: (0, 0))],
        out_specs=pl.BlockSpec((past + LANES, R), lambda i: (0, 0)),
        out_shape=jax.ShapeDtypeStruct((past + LANES, R), f32),
        scratch_shapes=[pltpu.VMEM((past + LANES, R), i32)],
        compiler_params=_cparams(("arbitrary",)), name="sample_select",
    )(scores_past, scores_new)


def _sattn_kernel(pt_ref, qbd_ref, sel_ref, seln_ref, bias_ref, biasn_ref, kn_ref, vn_ref, *rest, pp, tn):
    kpages, vpages = rest[:pp], rest[pp:2 * pp]
    o_ref, m_sc, l_sc, acc_sc = rest[2 * pp:]
    b = pl.program_id(0)
    p = pl.program_id(1)
    npg = pl.num_programs(1)
    rows = N_HEADS * tn
    qbd = qbd_ref[0]

    @pl.when(p == 0)
    def _():
        m_sc[...] = jnp.full(m_sc.shape, NEG_BIG, f32)
        l_sc[...] = jnp.zeros(l_sc.shape, f32)
        acc_sc[...] = jnp.zeros(acc_sc.shape, f32)

    r = lax.broadcasted_iota(i32, (LANES, LANES), 0)
    c = lax.broadcasted_iota(i32, (LANES, LANES), 1)
    expand = jnp.where((r // tn == b) & (r % tn == c % tn) & (c < rows), 1.0, 0.0).astype(bf16)

    def update(k_bf, v_bf, sel_t, bias_t):
        mask = jnp.dot(sel_t.astype(bf16), expand, preferred_element_type=f32)
        st = jnp.dot(k_bf, qbd, preferred_element_type=f32) + bias_t
        st = jnp.where(mask > 0.5, st, NEG_BIG)
        n = st.shape[0]
        s = jnp.concatenate([jnp.transpose(st[i * LANES:(i + 1) * LANES, :])[:rows] for i in range(n // LANES)],
                            axis=1)
        m_old = m_sc[...]
        m_new = jnp.maximum(m_old, jnp.max(s, axis=1, keepdims=True))
        pr = jnp.exp(s - m_new)
        alpha = jnp.exp(m_old - m_new)
        l_sc[...] = alpha * l_sc[...] + jnp.sum(pr, axis=1, keepdims=True)
        acc_sc[...] = alpha * acc_sc[...] + jnp.dot(pr.astype(bf16), v_bf, preferred_element_type=f32)
        m_sc[...] = m_new

    k_all = jnp.concatenate([kp[0].astype(bf16) for kp in kpages], axis=0)
    v_all = jnp.concatenate([vp[0].astype(bf16) for vp in vpages], axis=0)
    update(k_all, v_all, sel_ref[...], bias_ref[0])

    @pl.when(p == npg - 1)
    def _():
        update(kn_ref[0], vn_ref[0], seln_ref[...], biasn_ref[...])
        o_ref[0] = acc_sc[...] / l_sc[...]


def _sample_attention(page_table, q, k_new, v_new, sel_t, rel_bias, cache_k, cache_v, pp):
    Bd, Tn = q.shape[:2]
    n_pages = page_table.shape[1]
    ps = cache_k.shape[1]
    past = n_pages * ps
    d_attn = N_HEADS * HEAD_DIM
    rows = N_HEADS * Tn
    qt = jnp.transpose(q * (HEAD_DIM ** -0.5), (0, 2, 3, 1))
    qbd = (qt[:, :, :, None, :] * jnp.eye(N_HEADS, dtype=f32)[None, :, None, :, None]).reshape(Bd, d_attn, rows)
    qbd = jnp.pad(qbd, ((0, 0), (0, 0), (0, LANES - rows))).astype(bf16)
    rb = rel_bias.astype(f32)
    qpos = past + jnp.arange(Tn)
    lane_bias = lambda kpos: jnp.pad(
        jnp.transpose(rb[_rel_bucket(qpos[None, :] - kpos[:, None])], (0, 2, 1)).reshape(kpos.shape[0], rows),
        ((0, 0), (0, LANES - rows)))
    far = lane_bias(jnp.zeros((pp * ps,), i32))
    near = lane_bias(past - pp * ps + jnp.arange(pp * ps))
    bias_pages = jnp.stack([far, near])
    bias_new = lane_bias(past + jnp.arange(LANES))
    pad_new = lambda a: jnp.pad(a.reshape(Bd, Tn, d_attn), ((0, 0), (0, LANES - Tn), (0, 0))).astype(bf16)
    kn, vn = pad_new(k_new), pad_new(v_new)
    nsteps = n_pages // pp
    page_spec = lambda j: pl.BlockSpec((1, ps, d_attn), lambda b, p, pt: (pt[b, p * pp + j], 0, 0))
    grid_spec = pltpu.PrefetchScalarGridSpec(
        num_scalar_prefetch=1, grid=(Bd, nsteps),
        in_specs=[pl.BlockSpec((1, d_attn, LANES), lambda b, p, pt: (b, 0, 0)),
                  pl.BlockSpec((pp * ps, LANES), lambda b, p, pt: (p, 0)),
                  pl.BlockSpec((LANES, LANES), lambda b, p, pt: (past // LANES, 0)),
                  pl.BlockSpec((1, pp * ps, LANES), lambda b, p, pt: (jnp.where(p == nsteps - 1, 1, 0), 0, 0)),
                  pl.BlockSpec((LANES, LANES), lambda b, p, pt: (0, 0)),
                  pl.BlockSpec((1, LANES, d_attn), lambda b, p, pt: (b, 0, 0)),
                  pl.BlockSpec((1, LANES, d_attn), lambda b, p, pt: (b, 0, 0))]
                 + [page_spec(j) for j in range(pp)] * 2,
        out_specs=pl.BlockSpec((1, rows, d_attn), lambda b, p, pt: (b, 0, 0)),
        scratch_shapes=[pltpu.VMEM((rows, 1), f32), pltpu.VMEM((rows, 1), f32), pltpu.VMEM((rows, d_attn), f32)])
    out = pl.pallas_call(
        functools.partial(_sattn_kernel, pp=pp, tn=Tn), grid_spec=grid_spec,
        out_shape=jax.ShapeDtypeStruct((Bd, rows, d_attn), f32),
        compiler_params=_cparams(("parallel", "arbitrary")), name="sample_attn",
    )(page_table, qbd, sel_t, sel_t, bias_pages, bias_new, kn, vn, *([cache_k] * pp), *([cache_v] * pp))
    out = out.reshape(Bd, N_HEADS, Tn, N_HEADS, HEAD_DIM)
    hh = jnp.arange(N_HEADS)
    return jnp.transpose(out[:, hh, :, hh, :], (1, 2, 0, 3)).reshape(Bd, Tn, d_attn)


def _pick(n, cands):
    for c in cands:
        if n % c == 0:
            return c
    return n


def kernel(x_prompt, x_sample, cache_k, cache_v, cache_kidx, state_ssm_re, state_ssm_im, page_table, rel_bias, w_in, ssm_lambda_re, ssm_lambda_im, ssm_log_step, ssm_b_re, ssm_b_im, ssm_c_re, ssm_c_im, ssm_d, w_glu, w_attn_out, w_ssm_out, w_o, ln1_g, ln1_b, w_group_router, b_group_router, w_expert_router, b_expert_router, w_exp_gate, w_exp_up, w_exp_down, ln2_g, ln2_b):
    B, T, D = x_prompt.shape
    Bd, Tn = x_sample.shape[:2]
    n_phys, ps = cache_k.shape[1:3]
    d_attn = N_HEADS * HEAD_DIM
    d_qi = IDX_HEADS * IDX_DIM
    d_ssm = D // 2
    G = d_ssm // SSM_GROUP
    E = N_EXPERT_GROUPS * EXPERTS_PER_GROUP
    l = 0
    w = w_in[l]
    o_q, o_k, o_v, o_qi = 0, d_attn, 2 * d_attn, 3 * d_attn
    o_ki = o_qi + d_qi
    o_wi = o_ki + IDX_DIM
    o_u = o_wi + IDX_HEADS
    o_ga = o_u + d_ssm
    o_gb = o_ga + D
    cols = lambda a, n: w[:, a:a + n]

    small = jnp.concatenate([cols(o_ki, IDX_DIM), cols(o_wi, IDX_HEADS) * INDEXER_SCALE,
                             jnp.zeros((D, LANES - IDX_DIM - IDX_HEADS), f32)], axis=1)
    w_n = jnp.concatenate([cols(o_ga, D), cols(o_gb, D), cols(o_k, d_attn), cols(o_v, d_attn), small, cols(o_u, d_ssm)],
                          axis=1)
    c_ga, c_gb, c_k = 0, D, 2 * D
    c_v = c_k + d_attn
    c_s = c_v + d_attn
    c_u = c_s + LANES
    n_cols = c_u + d_ssm
    tn_n = 768
    w_n = jnp.pad(w_n, ((0, 0), (0, _round_up(n_cols, tn_n) - n_cols))).astype(bf16)
    w_t = jnp.concatenate([cols(o_q, d_attn) * (HEAD_DIM ** -0.5), cols(o_v, d_attn), cols(o_qi, d_qi)], axis=1).T.astype(bf16)
    w_wi = jnp.pad((cols(o_wi, IDX_HEADS) * INDEXER_SCALE).T, ((0, 2 * SUBLANES - IDX_HEADS), (0, 0))).astype(bf16)

    wa, wg, ws, wo = (a[l].astype(bf16) for a in (w_attn_out, w_glu, w_ssm_out, w_o))
    w_r = jnp.concatenate([w_expert_router[l], w_group_router[l], jnp.zeros((D, LANES - E - N_EXPERT_GROUPS), f32)], axis=1)
    b_r = jnp.concatenate([b_expert_router[l], b_group_router[l], jnp.zeros((LANES - E - N_EXPERT_GROUPS,), f32)])[None, :]
    weg, weu, wed = (a[l].astype(bf16) for a in (w_exp_gate, w_exp_up, w_exp_down))
    g1, b1, g2, b2 = (a[l][None, :].astype(f32) for a in (ln1_g, ln1_b, ln2_g, ln2_b))
    ssm_w = (ssm_lambda_re[l], ssm_lambda_im[l], ssm_log_step[l], ssm_b_re[l], ssm_b_im[l], ssm_c_re[l], ssm_c_im[l], ssm_d[l])

    def tail(x2, attn, ssm_y, zn):
        n = x2.shape[0]
        x1 = _merge(x2, attn, ssm_y, zn, c_ga // D, c_gb // D, wa, wg, ws, wo, g1, b1, _pick(n, (256, 128)))
        gates = _router(x1, w_r, b_r, _pick(n, (512, 128)))
        return _moe(x1, gates, weg, weu, wed, g2, b2, _pick(n, (512, 128)))

    N = B * T
    xp2 = x_prompt.reshape(N, D)
    xpb = xp2.astype(bf16)
    tm = _pick(N, (1024, 512, 256, 128))
    zn = _mm_nn(xpb, w_n, tm, tn_n)
    zt = _mm_nt(w_t, xpb, 512, tm, bf16)
    wit = _mm_nt(w_wi, xpb, 2 * SUBLANES, tm, f32)
    k_p, v_p, ki_p = zn[:, c_k:c_k + d_attn], zn[:, c_v:c_v + d_attn], zn[:, c_s:c_s + IDX_DIM]
    tq = _pick(T, (256, 128))
    attn_t = _prompt_attention(zt[:d_attn], zt[d_attn:2 * d_attn], zt[2 * d_attn:], wit,
                               ki_p.reshape(B, T, IDX_DIM).astype(bf16), k_p.astype(bf16),
                               _prompt_bias_tiles(rel_bias, tq), B, T, tq)
    L = _pick(T, (16, 8, 4, 2))
    zeros_state = jnp.zeros((B, G, STATE_DIM), f32)
    y_p, hr_p, hi_p = _ssm(zn[:, c_u:c_u + d_ssm].reshape(B, T, d_ssm), zeros_state, zeros_state,
                           _ssm_tables(*ssm_w, L), L)
    out_p = tail(xp2, attn_t.T, y_p.reshape(N, d_ssm), zn).reshape(B, T, D)

    Ns = Bd * Tn
    xs2 = x_sample.reshape(Ns, D)
    xsb = xs2.astype(bf16)
    zs = _mm_nn(xsb, w_n, Ns, tn_n)
    w_s = jnp.concatenate([cols(o_q, d_attn), cols(o_qi, d_qi)], axis=1)
    w_s = jnp.pad(w_s, ((0, 0), (0, _round_up(d_attn + d_qi, tn_n) - d_attn - d_qi))).astype(bf16)
    zq = _mm_nn(xsb, w_s, Ns, tn_n)
    q_s = zq[:, :d_attn].reshape(Bd, Tn, N_HEADS, HEAD_DIM)
    qi_s = zq[:, d_attn:d_attn + d_qi].reshape(Bd, Tn, IDX_HEADS, IDX_DIM)
    k_s = zs[:, c_k:c_k + d_attn].reshape(Bd, Tn, N_HEADS, HEAD_DIM)
    v_s = zs[:, c_v:c_v + d_attn].reshape(Bd, Tn, N_HEADS, HEAD_DIM)
    ki_s = zs[:, c_s:c_s + IDX_DIM].reshape(Bd, Tn, IDX_DIM)
    wi_s = zs[:, c_s + IDX_DIM:c_s + IDX_DIM + IDX_HEADS].reshape(Bd, Tn, IDX_HEADS)
    n_pages = page_table.shape[1]
    pp = _pick(n_pages, (4, 2, 1))
    sc_past, sc_new = _sample_scores(page_table, qi_s, wi_s, ki_s, cache_kidx[l].reshape(n_phys, ps, IDX_DIM), pp)
    sel_t = _sample_select(sc_past.reshape(Ns, n_pages * ps), sc_new.reshape(Ns, LANES), Tn)
    attn_s = _sample_attention(page_table, q_s, k_s, v_s, sel_t, rel_bias,
                               cache_k[l].reshape(n_phys, ps, d_attn), cache_v[l].reshape(n_phys, ps, d_attn), pp)
    y_s, hr_s, hi_s = _ssm(zs[:, c_u:c_u + d_ssm].reshape(Bd, Tn, d_ssm), state_ssm_re[l], state_ssm_im[l],
                           _ssm_tables(*ssm_w, Tn), Tn)
    out_s = tail(xs2, attn_s.reshape(Ns, d_attn), y_s.reshape(Ns, d_ssm), zs).reshape(Bd, Tn, D)

    sdt = state_ssm_re.dtype
    return (out_p, out_s,
            k_p.reshape(1, B, T // ps, ps, N_HEADS, HEAD_DIM), v_p.reshape(1, B, T // ps, ps, N_HEADS, HEAD_DIM),
            ki_p.reshape(1, B, T // ps, ps, IDX_DIM), hr_p.astype(sdt)[None], hi_p.astype(sdt)[None],
            k_s[None], v_s[None], ki_s[None], hr_s.astype(sdt)[None], hi_s.astype(sdt)[None])
```

```python
import functools
import math

import numpy as np
import jax
import jax.numpy as jnp
from jax import lax
from jax.experimental import pallas as pl
from jax.experimental.pallas import tpu as pltpu

f32, bf16, i32 = jnp.float32, jnp.bfloat16, jnp.int32

N_HEADS = 8
HEAD_DIM = 128
IDX_HEADS = 8
IDX_DIM = 64
TOPK_MAX = 256
NUM_BUCKETS = 32
MAX_DISTANCE = 128
SSM_GROUP = 16
STATE_DIM = 64
N_EXPERT_GROUPS = 4
EXPERTS_PER_GROUP = 8
TOP_K_EXPERT = 2
DEPTH = 1
ALPHA = (2 * DEPTH) ** 0.25
LN_EPS = 1e-5
INDEXER_SCALE = (IDX_HEADS ** -0.5) * (IDX_DIM ** -0.5)

LANES = 128
SUBLANES = 8
VMEM_LIMIT = 56 * 1024 * 1024

INT_MIN = np.int32(-2 ** 31)
KEY_NEG_INF = np.int32(-2139095041)
NEG_BIG = -1e30


def _cparams(sem):
    return pltpu.CompilerParams(dimension_semantics=sem, vmem_limit_bytes=VMEM_LIMIT)


def _round_up(a, b):
    return (a + b - 1) // b * b


def _mm_nn_kernel(a_ref, b_ref, o_ref):
    o_ref[...] = jnp.dot(a_ref[...], b_ref[...], preferred_element_type=f32).astype(o_ref.dtype)


def _mm_nn(a, b, tm, tn, out_dtype=f32):
    M, K = a.shape
    N = b.shape[1]
    return pl.pallas_call(
        _mm_nn_kernel, grid=(M // tm, N // tn),
        in_specs=[pl.BlockSpec((tm, K), lambda i, j: (i, 0)), pl.BlockSpec((K, tn), lambda i, j: (0, j))],
        out_specs=pl.BlockSpec((tm, tn), lambda i, j: (i, j)),
        out_shape=jax.ShapeDtypeStruct((M, N), out_dtype),
        compiler_params=_cparams(("parallel", "arbitrary")), name="proj_nn")(a, b)


def _mm_nt_kernel(w_ref, x_ref, o_ref):
    o_ref[...] = lax.dot_general(w_ref[...], x_ref[...], (((1,), (1,)), ((), ())),
                                 preferred_element_type=f32).astype(o_ref.dtype)


def _mm_nt(w, x, tn, tm, out_dtype):
    n, K = w.shape
    M = x.shape[0]
    return pl.pallas_call(
        _mm_nt_kernel, grid=(M // tm, n // tn),
        in_specs=[pl.BlockSpec((tn, K), lambda i, j: (j, 0)), pl.BlockSpec((tm, K), lambda i, j: (i, 0))],
        out_specs=pl.BlockSpec((tn, tm), lambda i, j: (j, i)),
        out_shape=jax.ShapeDtypeStruct((n, M), out_dtype),
        compiler_params=_cparams(("parallel", "arbitrary")), name="proj_nt")(w, x)


def _order_key(x):
    b = lax.bitcast_convert_type(x + 0.0, i32)
    return b ^ ((b >> 31) & np.int32(0x7FFFFFFF))


def _bucket_starts():
    me = NUM_BUCKETS // 2
    d = np.arange(1, 4 * MAX_DISTANCE)
    large = me + (np.log(d.astype(np.float32) / np.float32(me)) / np.float32(math.log(MAX_DISTANCE / me))
                  * np.float32(NUM_BUCKETS - me)).astype(np.int32)
    bucket = np.where(d < me, d, np.minimum(large, NUM_BUCKETS - 1))
    return [int(d[np.argmax(bucket >= b)]) for b in range(me + 1, NUM_BUCKETS)]


def _rel_bucket(dist):
    me = NUM_BUCKETS // 2
    d = jnp.maximum(dist, 0)
    large = me
    for start in _bucket_starts():
        large = large + jnp.where(d >= start, 1, 0)
    return jnp.where(d < me, d, large)


def _select_rows(keys_sc, n_rows, k_top, idx_bits, slab):
    tq = keys_sc.shape[1]
    n_slabs = n_rows // slab

    def count(pred):
        def body(i, cnt):
            r0 = pl.multiple_of(i * slab, slab)
            blk = keys_sc[pl.ds(r0, slab), :]
            hit = jnp.where(pred(blk, r0), 1, 0).astype(i32)
            return cnt + jnp.sum(hit.reshape(slab // SUBLANES, SUBLANES, tq), axis=0)
        cnt = lax.fori_loop(0, n_slabs, body, jnp.zeros((SUBLANES, tq), i32))
        return jnp.sum(cnt, axis=0, keepdims=True)

    def count_ge(cand):
        cb = jnp.broadcast_to(cand, (slab, tq))
        return count(lambda blk, r0: blk >= cb)

    zero = jnp.zeros((1, tq), i32)
    thr = jnp.where(count_ge(zero) >= k_top, zero, jnp.full((1, tq), INT_MIN, i32))

    def bit_body(it, thr):
        cand = thr | lax.shift_left(np.int32(1), np.int32(30) - it)
        return jnp.where(count_ge(cand) >= k_top, cand, thr)

    thr = lax.fori_loop(0, 31, bit_body, thr)
    thr = jnp.maximum(thr, KEY_NEG_INF)
    n_ge = count_ge(thr)

    @pl.when(jnp.max(n_ge) > k_top)
    def _():
        need = k_top - count_ge(thr + 1)
        tb = jnp.broadcast_to(thr, (slab, tq))

        def count_eq_below(m):
            mb = jnp.broadcast_to(m, (slab, tq))
            return count(lambda blk, r0: (blk == tb) & (r0 + lax.broadcasted_iota(i32, (slab, tq), 0) < mb))

        def idx_body(it, m):
            cand = m | lax.shift_left(np.int32(1), np.int32(idx_bits - 1) - it)
            return jnp.where(count_eq_below(cand) < need, cand, m)

        m = lax.fori_loop(0, idx_bits, idx_body, zero)
        mb = jnp.broadcast_to(m, (slab, tq))

        def demote(i, _):
            r0 = pl.multiple_of(i * slab, slab)
            blk = keys_sc[pl.ds(r0, slab), :]
            drop = (blk == tb) & (r0 + lax.broadcasted_iota(i32, (slab, tq), 0) > mb)
            keys_sc[pl.ds(r0, slab), :] = jnp.where(drop, blk - 1, blk)
            return 0

        lax.fori_loop(0, n_slabs, demote, 0)

    return thr


def _pattn_kernel(qiT_ref, wiT_ref, ki_ref, qT_ref, k_ref, vT_ref, bias_ref, oT_ref,
                  keys_sc, thr_sc, m_sc, l_sc, acc_sc, *, k_top, tq, idx_bits):
    qb = pl.program_id(1)
    kb = pl.program_id(2)
    nk = pl.num_programs(2)

    @pl.when(kb == 0)
    def _():
        def chunk(c, _):
            r0 = pl.multiple_of(c * tq, tq)
            ki_c = ki_ref[0, pl.ds(r0, tq), :]
            score = jnp.zeros((tq, tq), f32)
            for h in range(IDX_HEADS):
                s = jnp.dot(ki_c, qiT_ref[h * IDX_DIM:(h + 1) * IDX_DIM, :], preferred_element_type=f32)
                score = score + wiT_ref[h:h + 1, :] * jnp.maximum(s, 0.0)
            kpos = r0 + lax.broadcasted_iota(i32, (tq, tq), 0)
            qpos = qb * tq + lax.broadcasted_iota(i32, (tq, tq), 1)
            keys_sc[pl.ds(r0, tq), :] = jnp.where(kpos <= qpos, _order_key(score), INT_MIN)
            return 0

        lax.fori_loop(0, qb + 1, chunk, 0)
        thr_sc[...] = _select_rows(keys_sc, (qb + 1) * tq, k_top, idx_bits, 64)
        m_sc[...] = jnp.full(m_sc.shape, NEG_BIG, f32)
        l_sc[...] = jnp.zeros(l_sc.shape, f32)
        acc_sc[...] = jnp.zeros(acc_sc.shape, f32)

    @pl.when(kb <= qb)
    def _():
        r0 = pl.multiple_of(kb * tq, tq)
        sel = keys_sc[pl.ds(r0, tq), :] >= thr_sc[...]
        for h in range(N_HEADS):
            hs = slice(h * HEAD_DIM, (h + 1) * HEAD_DIM)
            s = jnp.dot(k_ref[:, hs], qT_ref[hs, :], preferred_element_type=f32) + bias_ref[h, 0]
            s = jnp.where(sel, s, NEG_BIG)
            m_old = m_sc[h:h + 1, :]
            m_new = jnp.maximum(m_old, jnp.max(s, axis=0, keepdims=True))
            p = jnp.where(sel, jnp.exp(s - m_new), 0.0)
            alpha = jnp.exp(m_old - m_new)
            l_sc[h:h + 1, :] = alpha * l_sc[h:h + 1, :] + jnp.sum(p, axis=0, keepdims=True)
            acc_sc[hs, :] = alpha * acc_sc[hs, :] + jnp.dot(vT_ref[hs, :], p.astype(bf16),
                                                            preferred_element_type=f32)
            m_sc[h:h + 1, :] = m_new

    @pl.when(kb == nk - 1)
    def _():
        for h in range(N_HEADS):
            hs = slice(h * HEAD_DIM, (h + 1) * HEAD_DIM)
            oT_ref[hs, :] = acc_sc[hs, :] / l_sc[h:h + 1, :]


def _prompt_attention(qT, vT, qiT, wiT, ki, k, bias_tiles, B, T, tq):
    nq = T // tq
    k_top = min(TOPK_MAX, T // 4)
    d_attn = N_HEADS * HEAD_DIM
    kern = functools.partial(_pattn_kernel, k_top=k_top, tq=tq, idx_bits=max(1, (T - 1).bit_length()))
    col = lambda b, q, kb: (0, b * nq + q)
    kcl = lambda b, q, kb: jnp.minimum(kb, q)
    return pl.pallas_call(
        kern, grid=(B, nq, nq),
        in_specs=[
            pl.BlockSpec((IDX_HEADS * IDX_DIM, tq), col),
            pl.BlockSpec((IDX_HEADS, tq), col),
            pl.BlockSpec((1, T, IDX_DIM), lambda b, q, kb: (b, 0, 0)),
            pl.BlockSpec((d_attn, tq), col),
            pl.BlockSpec((tq, d_attn), lambda b, q, kb: (b * nq + kcl(b, q, kb), 0)),
            pl.BlockSpec((d_attn, tq), lambda b, q, kb: (0, b * nq + kcl(b, q, kb))),
            pl.BlockSpec((N_HEADS, 1, tq, tq), lambda b, q, kb: (0, jnp.clip(q - kb, 0, 2), 0, 0)),
        ],
        out_specs=pl.BlockSpec((d_attn, tq), col),
        out_shape=jax.ShapeDtypeStruct((d_attn, B * T), f32),
        scratch_shapes=[pltpu.VMEM((T, tq), i32), pltpu.VMEM((1, tq), i32),
                        pltpu.VMEM((N_HEADS, tq), f32), pltpu.VMEM((N_HEADS, tq), f32),
                        pltpu.VMEM((d_attn, tq), f32)],
        compiler_params=_cparams(("parallel", "arbitrary", "arbitrary")), name="prompt_attn",
    )(qiT, wiT, ki, qT, k, vT, bias_tiles)


def _bias_lookup_kernel(rb_ref, bkt_ref, o_ref):
    bkt = bkt_ref[0]
    for h in range(N_HEADS):
        acc = jnp.zeros(bkt.shape, f32)
        for b in range(NUM_BUCKETS):
            acc = jnp.where(bkt == b, rb_ref[b, h], acc)
        o_ref[h, 0] = acc


def _prompt_bias_tiles(rel_bias, tq):
    s = jnp.arange(tq, dtype=i32)[:, None]
    t = jnp.arange(tq, dtype=i32)[None, :]
    bkt = jnp.stack([_rel_bucket(d * tq + t - s) for d in range(3)]).astype(i32)
    return pl.pallas_call(
        _bias_lookup_kernel, grid=(3,),
        in_specs=[pl.BlockSpec(memory_space=pltpu.SMEM), pl.BlockSpec((1, tq, tq), lambda d: (d, 0, 0))],
        out_specs=pl.BlockSpec((N_HEADS, 1, tq, tq), lambda d: (0, d, 0, 0)),
        out_shape=jax.ShapeDtypeStruct((N_HEADS, 3, tq, tq), f32),
        compiler_params=_cparams(("parallel",)), name="bias_tiles")(rel_bias.astype(f32), bkt)


def _ssm_tables(lam_re, lam_im, log_step, b_re, b_im, c_re, c_im, d, L):
    hp = lax.Precision.HIGHEST
    G, P = lam_re.shape
    H = b_re.shape[-1]
    lr, li = lam_re.astype(f32), lam_im.astype(f32)
    step = jnp.exp(log_step.astype(f32))[:, None]
    taus = jnp.arange(L + 1, dtype=f32)[:, None, None]
    mag = jnp.exp(lr * step * taus)
    pr, pi = mag * jnp.cos(li * step * taus), mag * jnp.sin(li * step * taus)
    x, y, den = pr[1] - 1.0, pi[1], lr * lr + li * li
    fr, fi = (x * lr + y * li) / den, (y * lr - x * li) / den
    bbr = fr[..., None] * b_re - fi[..., None] * b_im
    bbi = fr[..., None] * b_im + fi[..., None] * b_re
    clr = c_re[None] * pr[:, :, None, :] - c_im[None] * pi[:, :, None, :]
    cli = c_re[None] * pi[:, :, None, :] + c_im[None] * pr[:, :, None, :]
    kern = (jnp.einsum('tghp,gpk->tghk', clr[:L], bbr, precision=hp)
            - jnp.einsum('tghp,gpk->tghk', cli[:L], bbi, precision=hp))
    tt = jnp.arange(L)[:, None] - jnp.arange(L)[None, :]
    toe = jnp.where((tt >= 0)[:, :, None, None, None], kern[jnp.maximum(tt, 0)], 0.0)
    toe = toe + (jnp.eye(L)[:, :, None, None, None] * jnp.eye(H)[None, None, None] * d.astype(f32)[None, None, :, :, None])
    GB = LANES // H
    J = G // GB
    eye = jnp.eye(GB, dtype=f32)
    mT = jnp.einsum('tsjghk,gf->jsgktfh', toe.reshape(L, L, J, GB, H, H), eye).reshape(J, L * LANES, L * LANES)
    rev = jnp.arange(L - 1, -1, -1)
    wr = pr[rev][:, :, :, None] * bbr[None] - pi[rev][:, :, :, None] * bbi[None]
    wi = pr[rev][:, :, :, None] * bbi[None] + pi[rev][:, :, :, None] * bbr[None]
    to_w = lambda a: jnp.einsum('sjgpk,gf->jsgkfp', a.reshape(L, J, GB, P, H), eye).reshape(J, L * LANES, GB * P)
    w = jnp.concatenate([to_w(wr), to_w(wi)], axis=2)
    to_v = lambda a: jnp.einsum('tjghp,gf->jgptfh', a.reshape(L, J, GB, H, P), eye).reshape(J, GB * P, L * LANES)
    v = jnp.concatenate([to_v(clr[1:]), -to_v(cli[1:])], axis=1)
    lam = jnp.concatenate([pr[L].reshape(J, 1, GB * P), pi[L].reshape(J, 1, GB * P)], axis=2)
    return mT.astype(bf16), w.astype(bf16), v.astype(bf16), lam


def _ssm_kernel(u_ref, mt_ref, w_ref, v_ref, lam_ref, h0_ref, y_ref, hout_ref, uc_sc, s_sc, hin_sc, yc_sc,
                *, L, rows, seq):
    S = s_sc.shape[1] // 2
    for s in range(L):
        uc_sc[:, s * LANES:(s + 1) * LANES] = u_ref[pl.ds(s, rows, stride=L), :].astype(bf16)
    uc = uc_sc[...]
    s_sc[...] = jnp.dot(uc, w_ref[0], preferred_element_type=f32)
    lr, li = lam_ref[0, :, 0:S], lam_ref[0, :, S:2 * S]
    if seq:
        def body(c, carry):
            hr, hi = carry
            hin_sc[pl.ds(c, 1), 0:S] = hr
            hin_sc[pl.ds(c, 1), S:2 * S] = hi
            sr, si = s_sc[pl.ds(c, 1), 0:S], s_sc[pl.ds(c, 1), S:2 * S]
            return lr * hr - li * hi + sr, lr * hi + li * hr + si

        hr, hi = lax.fori_loop(0, rows, body, (h0_ref[0, 0, :, 0:S], h0_ref[0, 0, :, S:2 * S]))
    else:
        hr0, hi0 = h0_ref[0, 0, :, 0:S], h0_ref[0, 0, :, S:2 * S]
        hin_sc[...] = h0_ref[0, 0]
        hr = lr * hr0 - li * hi0 + s_sc[:, 0:S]
        hi = lr * hi0 + li * hr0 + s_sc[:, S:2 * S]
    hout_ref[0, 0, :, 0:S] = hr
    hout_ref[0, 0, :, S:2 * S] = hi
    yc_sc[...] = (jnp.dot(uc, mt_ref[0], preferred_element_type=f32)
                  + jnp.dot(hin_sc[...].astype(bf16), v_ref[0], preferred_element_type=f32))
    for t in range(L):
        y_ref[pl.ds(t, rows, stride=L), :] = yc_sc[:, t * LANES:(t + 1) * LANES]


def _ssm(zn, u_blk0, d_ssm, n_steps, h0, tables, L, seq):
    mT, w, v, lam = tables
    N = zn.shape[0]
    J = d_ssm // LANES
    tok = N // n_steps
    rows = tok // L
    rows_h = h0.shape[2]
    S2 = h0.shape[3]
    kern = functools.partial(_ssm_kernel, L=L, rows=rows, seq=seq)
    return pl.pallas_call(
        kern, grid=(J, n_steps),
        in_specs=[
            pl.BlockSpec((tok, LANES), lambda j, b: (b, u_blk0 + j)),
            pl.BlockSpec((1, L * LANES, L * LANES), lambda j, b: (j, 0, 0)),
            pl.BlockSpec((1, L * LANES, S2), lambda j, b: (j, 0, 0)),
            pl.BlockSpec((1, S2, L * LANES), lambda j, b: (j, 0, 0)),
            pl.BlockSpec((1, 1, S2), lambda j, b: (j, 0, 0)),
            pl.BlockSpec((1, 1, rows_h, S2), lambda j, b: (j, b, 0, 0)),
        ],
        out_specs=[pl.BlockSpec((tok, LANES), lambda j, b: (b, j)),
                   pl.BlockSpec((1, 1, rows_h, S2), lambda j, b: (j, b, 0, 0))],
        out_shape=[jax.ShapeDtypeStruct((N, d_ssm), f32),
                   jax.ShapeDtypeStruct(h0.shape, f32)],
        scratch_shapes=[pltpu.VMEM((rows, L * LANES), bf16), pltpu.VMEM((rows, S2), f32),
                        pltpu.VMEM((rows, S2), f32), pltpu.VMEM((rows, L * LANES), f32)],
        compiler_params=_cparams(("parallel", "arbitrary")), name="ssm",
    )(zn, mT, w, v, lam, h0)


def _pack_state(a_re, a_im):
    Bn, G, P = a_re.shape
    J = G * SSM_GROUP // LANES
    blk = lambda a: jnp.transpose(a.astype(f32).reshape(Bn, J, (G // J) * P), (1, 0, 2))
    return jnp.concatenate([blk(a_re), blk(a_im)], axis=2)


def _unpack_state(h, G, P):
    J, Bn, S2 = h.shape
    un = lambda a: jnp.transpose(a, (1, 0, 2)).reshape(Bn, G, P)
    return un(h[:, :, :S2 // 2]), un(h[:, :, S2 // 2:])


def _sigmoid(x):
    return 1.0 / (1.0 + jnp.exp(-x))


def _gelu_tanh(x):
    return 0.5 * x * (1.0 + jnp.tanh(math.sqrt(2.0 / math.pi) * (x + 0.044715 * (x * x * x))))


def _layer_norm(r, g, b):
    mu = jnp.mean(r, axis=-1, keepdims=True)
    c = r - mu
    var = jnp.mean(c * c, axis=-1, keepdims=True)
    return c * lax.rsqrt(var + LN_EPS) * g + b


def _merge_kernel(x_ref, attn_ref, y_ref, ga_ref, gb_ref, wa_ref, wg_ref, ws_ref, wo_ref, g_ref, b_ref, o_ref):
    a = jnp.dot(attn_ref[...].astype(bf16), wa_ref[...], preferred_element_type=f32)
    z = _gelu_tanh(y_ref[...])
    gl = jnp.dot(z.astype(bf16), wg_ref[...], preferred_element_type=f32)
    s = jnp.dot((z * _sigmoid(gl)).astype(bf16), ws_ref[...], preferred_element_type=f32)
    m = _sigmoid(ga_ref[...]) * a + _sigmoid(gb_ref[...]) * s
    mix = jnp.dot(m.astype(bf16), wo_ref[...], preferred_element_type=f32)
    o_ref[...] = _layer_norm(ALPHA * x_ref[...] + mix, g_ref[...], b_ref[...])


def _merge(x, attn, ssm_y, zn, ga_blk, gb_blk, wa, wg, ws, wo, g, b, tm):
    N, D = x.shape
    Dh = attn.shape[1]
    const = lambda shape: pl.BlockSpec(shape, lambda i: (0, 0), pipeline_mode=pl.Buffered(1))
    return pl.pallas_call(
        _merge_kernel, grid=(N // tm,),
        in_specs=[pl.BlockSpec((tm, D), lambda i: (i, 0)),
                  pl.BlockSpec((tm, Dh), lambda i: (i, 0)),
                  pl.BlockSpec((tm, Dh), lambda i: (i, 0)),
                  pl.BlockSpec((tm, D), lambda i: (i, ga_blk)),
                  pl.BlockSpec((tm, D), lambda i: (i, gb_blk)),
                  const((Dh, D)), const((Dh, Dh)), const((Dh, D)), const((D, D)),
                  const((1, D)), const((1, D))],
        out_specs=pl.BlockSpec((tm, D), lambda i: (i, 0)),
        out_shape=jax.ShapeDtypeStruct((N, D), f32),
        compiler_params=_cparams(("parallel",)), name="merge_ln1",
    )(x, attn, ssm_y, zn, zn, wa, wg, ws, wo, g, b)


def _router_kernel(x_ref, w_ref, b_ref, o_ref):
    E = N_EXPERT_GROUPS * EXPERTS_PER_GROUP
    logit = jnp.dot(x_ref[...], w_ref[...], preferred_element_type=f32, precision=lax.Precision.HIGHEST) + b_ref[...]
    lane = lax.broadcasted_iota(i32, logit.shape, 1)
    is_g = (lane >= E) & (lane < E + N_EXPERT_GROUPS)
    glog = jnp.where(is_g, logit, -jnp.inf)
    g_max = jnp.max(glog, axis=1, keepdims=True)
    g_idx = jnp.min(jnp.where(glog == g_max, lane, 4 * LANES), axis=1, keepdims=True) - E
    g_w = 1.0 / jnp.sum(jnp.where(is_g, jnp.exp(glog - g_max), 0.0), axis=1, keepdims=True)
    in_g = (lane < E) & (lane // EXPERTS_PER_GROUP == g_idx)
    e1 = jnp.where(in_g, logit, -jnp.inf)
    v1 = jnp.max(e1, axis=1, keepdims=True)
    i1 = jnp.min(jnp.where(e1 == v1, lane, 4 * LANES), axis=1, keepdims=True)
    e2 = jnp.where(lane == i1, -jnp.inf, e1)
    v2 = jnp.max(e2, axis=1, keepdims=True)
    i2 = jnp.min(jnp.where(e2 == v2, lane, 4 * LANES), axis=1, keepdims=True)
    t = jnp.exp(v2 - v1)
    w1 = g_w / (1.0 + t)
    w2 = g_w * t / (1.0 + t)
    o_ref[...] = jnp.where(lane == i1, w1, 0.0) + jnp.where(lane == i2, w2, 0.0)


def _router(x1, w_r, b_r, tm):
    N, D = x1.shape
    return pl.pallas_call(
        _router_kernel, grid=(N // tm,),
        in_specs=[pl.BlockSpec((tm, D), lambda i: (i, 0)),
                  pl.BlockSpec((D, LANES), lambda i: (0, 0)),
                  pl.BlockSpec((1, LANES), lambda i: (0, 0))],
        out_specs=pl.BlockSpec((tm, LANES), lambda i: (i, 0)),
        out_shape=jax.ShapeDtypeStruct((N, LANES), f32),
        compiler_params=_cparams(("parallel",)), name="router",
    )(x1, w_r, b_r)


def _moe_kernel(x_ref, gate_ref, wg_ref, wu_ref, wd_ref, g_ref, b_ref, o_ref, xb_sc, acc_sc):
    e = pl.program_id(1)

    @pl.when(e == 0)
    def _():
        xb_sc[...] = x_ref[...].astype(bf16)
        acc_sc[...] = jnp.zeros(acc_sc.shape, f32)

    gates = gate_ref[...]
    lane = lax.broadcasted_iota(i32, gates.shape, 1)
    ge = jnp.sum(jnp.where(lane == e, gates, 0.0), axis=1, keepdims=True)
    xb = xb_sc[...]
    hg = jnp.dot(xb, wg_ref[0], preferred_element_type=f32)
    hu = jnp.dot(xb, wu_ref[0], preferred_element_type=f32)
    h = hg * _sigmoid(hg) * hu
    acc_sc[...] += jnp.dot((h * ge).astype(bf16), wd_ref[0], preferred_element_type=f32)

    @pl.when(e == pl.num_programs(1) - 1)
    def _():
        o_ref[...] = _layer_norm(ALPHA * x_ref[...] + acc_sc[...], g_ref[...], b_ref[...])


def _moe(x1, gates, wg, wu, wd, g, b, tm):
    N, D = x1.shape
    E, _, De = wg.shape
    return pl.pallas_call(
        _moe_kernel, grid=(N // tm, E),
        in_specs=[pl.BlockSpec((tm, D), lambda i, e: (i, 0)),
                  pl.BlockSpec((tm, LANES), lambda i, e: (i, 0)),
                  pl.BlockSpec((1, D, De), lambda i, e: (e, 0, 0)),
                  pl.BlockSpec((1, D, De), lambda i, e: (e, 0, 0)),
                  pl.BlockSpec((1, De, D), lambda i, e: (e, 0, 0)),
                  pl.BlockSpec((1, D), lambda i, e: (0, 0)),
                  pl.BlockSpec((1, D), lambda i, e: (0, 0))],
        out_specs=pl.BlockSpec((tm, D), lambda i, e: (i, 0)),
        out_shape=jax.ShapeDtypeStruct((N, D), f32),
        scratch_shapes=[pltpu.VMEM((tm, D), bf16), pltpu.VMEM((tm, D), f32)],
        compiler_params=_cparams(("parallel", "arbitrary")), name="moe_ln2",
    )(x1, gates, wg, wu, wd, g, b)


def _sscore_kernel(pt_ref, qi_ref, wi_ref, kn_ref, *rest, pp):
    pages, (o_ref, on_ref) = rest[:pp], rest[pp:]
    qi = qi_ref[0]
    wi = wi_ref[0]
    tn = qi.shape[0] // IDX_HEADS

    def scores(keys):
        s = lax.dot_general(qi, keys, (((1,), (1,)), ((), ())), preferred_element_type=f32)
        r = jnp.maximum(s, 0.0) * wi
        return jnp.sum(r.reshape(tn, IDX_HEADS, keys.shape[0]), axis=1)

    for j in range(pp):
        o_ref[0, :, j * LANES:(j + 1) * LANES] = scores(pages[j][0].astype(bf16))

    @pl.when(pl.program_id(1) == 0)
    def _():
        on_ref[0] = scores(kn_ref[0])


def _sample_scores(page_table, qi, wi, ki_new, cache_kidx, pp):
    Bd, Tn = qi.shape[:2]
    n_pages = page_table.shape[1]
    ps = cache_kidx.shape[1]
    rows = Tn * IDX_HEADS
    qi2 = qi.reshape(Bd, rows, IDX_DIM).astype(bf16)
    wi2 = wi.reshape(Bd, rows, 1)
    kn = jnp.pad(ki_new, ((0, 0), (0, LANES - Tn), (0, 0))).astype(bf16)
    page_spec = lambda j: pl.BlockSpec((1, ps, IDX_DIM), lambda b, p, pt: (pt[b, p * pp + j], 0, 0))
    grid_spec = pltpu.PrefetchScalarGridSpec(
        num_scalar_prefetch=1, grid=(Bd, n_pages // pp),
        in_specs=[pl.BlockSpec((1, rows, IDX_DIM), lambda b, p, pt: (b, 0, 0)),
                  pl.BlockSpec((1, rows, 1), lambda b, p, pt: (b, 0, 0)),
                  pl.BlockSpec((1, LANES, IDX_DIM), lambda b, p, pt: (b, 0, 0))]
                 + [page_spec(j) for j in range(pp)],
        out_specs=[pl.BlockSpec((1, Tn, pp * ps), lambda b, p, pt: (b, 0, p)),
                   pl.BlockSpec((1, Tn, LANES), lambda b, p, pt: (b, 0, 0))])
    return pl.pallas_call(
        functools.partial(_sscore_kernel, pp=pp), grid_spec=grid_spec,
        out_shape=[jax.ShapeDtypeStruct((Bd, Tn, n_pages * ps), f32),
                   jax.ShapeDtypeStruct((Bd, Tn, LANES), f32)],
        compiler_params=_cparams(("parallel", "arbitrary")), name="sample_scores",
    )(page_table, qi2, wi2, kn, *([cache_kidx] * pp))


def _sselect_kernel(sp_ref, sn_ref, o_ref, keys_sc, *, k_top, tn, idx_bits):
    rows = sp_ref.shape[0]
    past = sp_ref.shape[1]
    n_tiles = past // LANES

    def load(j, _):
        c0 = pl.multiple_of(j * LANES, LANES)
        keys_sc[pl.ds(c0, LANES), :] = _order_key(jnp.transpose(sp_ref[:, pl.ds(c0, LANES)]))
        return 0

    lax.fori_loop(0, n_tiles, load, 0)
    new = _order_key(jnp.transpose(sn_ref[...]))
    jj = lax.broadcasted_iota(i32, new.shape, 0)
    qq = lax.broadcasted_iota(i32, new.shape, 1) % tn
    keys_sc[pl.ds(past, LANES), :] = jnp.where(jj <= qq, new, INT_MIN)
    thr = _select_rows(keys_sc, past + LANES, k_top, idx_bits, 64)

    none = np.int32(2 ** 30)
    o_ref[...] = jnp.zeros(o_ref.shape, i32)
    slot = lax.broadcasted_iota(i32, o_ref.shape, 0)

    def slab(j, filled):
        c0 = pl.multiple_of(j * LANES, LANES)
        sel = keys_sc[pl.ds(c0, LANES), :] >= thr
        cand = jnp.where(sel, c0 + lax.broadcasted_iota(i32, sel.shape, 0), none)
        n_sel = jnp.sum(jnp.where(sel, 1, 0).astype(i32), axis=0, keepdims=True)

        def extract(i, cand):
            first = jnp.min(cand, axis=0, keepdims=True)
            o_ref[...] = jnp.where((slot == filled + i) & (first < none), first, o_ref[...])
            return jnp.where(cand == first, none, cand)

        lax.fori_loop(0, jnp.max(n_sel), extract, cand)
        return filled + n_sel

    lax.fori_loop(0, n_tiles + 1, slab, jnp.zeros((1, rows), i32))


def _sample_select(scores_past, scores_new, tn):
    assert scores_past.shape[0] <= LANES
    pad_rows = lambda a: jnp.pad(a, ((0, LANES - a.shape[0]), (0, 0)))
    scores_past, scores_new = pad_rows(scores_past), pad_rows(scores_new)
    R, past = scores_past.shape
    k_top = min(TOPK_MAX, (past + tn) // 4)
    assert past + 1 >= k_top
    kern = functools.partial(_sselect_kernel, k_top=k_top, tn=tn, idx_bits=(past + LANES - 1).bit_length())
    return pl.pallas_call(
        kern, grid=(1,),
        in_specs=[pl.BlockSpec((R, past), lambda i: (0, 0)), pl.BlockSpec((R, LANES), lambda i: (0, 0))],
        out_specs=pl.BlockSpec((k_top, R), lambda i: (0, 0)),
        out_shape=jax.ShapeDtypeStruct((k_top, R), i32),
        scratch_shapes=[pltpu.VMEM((past + LANES, R), i32)],
        compiler_params=_cparams(("arbitrary",)), name="sample_select",
    )(scores_past, scores_new)


def _sgather_kernel(idx_ref, pt_ref, qT_ref, idxv_ref, knew_ref, vnew_ref, rb_ref, ck_hbm, cv_hbm, o_ref,
                    kbuf, vbuf, ksem, vsem, *, K, tn, past, n_pages, ps):
    r = pl.program_id(0)
    n = pl.num_programs(0)
    slot = r % 2

    def row_copies(rr, sl, j):
        idx = jnp.minimum(idx_ref[rr * K + j], past - 1)
        phys = pt_ref[(rr // tn) * n_pages + idx // ps]
        return (pltpu.make_async_copy(ck_hbm.at[phys, idx % ps], kbuf.at[sl, j], ksem.at[sl]),
                pltpu.make_async_copy(cv_hbm.at[phys, idx % ps], vbuf.at[sl, j], vsem.at[sl]))

    def issue(rr, sl):
        def body(j, _):
            ck, cv = row_copies(rr, sl, j)
            ck.start()
            cv.start()
            return 0
        lax.fori_loop(0, K, body, 0)

    @pl.when(r == 0)
    def _():
        issue(0, 0)

    @pl.when(r + 1 < n)
    def _():
        issue(r + 1, 1 - slot)

    def wait_body(j, _):
        ck, cv = row_copies(r, slot, j)
        ck.wait()
        cv.wait()
        return 0

    lax.fori_loop(0, K, wait_body, 0)

    for t in range(tn):
        j = K - 1 - t
        pos = idx_ref[r * K + j]

        @pl.when(pos >= past)
        def _():
            kbuf[slot, j] = knew_ref[0, pos - past]
            vbuf[slot, j] = vnew_ref[0, pos - past]

    idxv = idxv_ref[0]
    k = kbuf[slot]
    v = vbuf[slot]
    pairs = jnp.dot(k.reshape(K * N_HEADS, HEAD_DIM).astype(bf16), qT_ref[0], preferred_element_type=f32)
    pairs = pairs.reshape(K, N_HEADS, LANES)
    diag = lax.broadcasted_iota(i32, pairs.shape, 1) == lax.broadcasted_iota(i32, pairs.shape, 2)
    logits = jnp.sum(jnp.where(diag, pairs, 0.0), axis=1)
    dist = past + r % tn - idxv
    onehot = jnp.where(_rel_bucket(dist) == lax.broadcasted_iota(i32, (K, LANES), 1), 1.0, 0.0)
    logits = logits + jnp.dot(onehot, rb_ref[...], preferred_element_type=f32, precision=lax.Precision.HIGHEST)
    p = jnp.exp(logits - jnp.max(logits, axis=0, keepdims=True))
    p = p / jnp.sum(p, axis=0, keepdims=True)
    pd = jnp.where(diag, jnp.broadcast_to(p[:, None, :], pairs.shape), 0.0).reshape(K * N_HEADS, LANES)
    spread = jnp.dot(pd.astype(bf16), jnp.ones((LANES, HEAD_DIM), bf16), preferred_element_type=f32)
    o_ref[0] = jnp.sum(spread.reshape(K, N_HEADS, HEAD_DIM) * v, axis=0)


def _sample_attention(page_table, q, k_new, v_new, idx_list, rel_bias, cache_k, cache_v):
    Bd, Tn = q.shape[:2]
    n_pages = page_table.shape[1]
    ps = cache_k.shape[1]
    past = n_pages * ps
    R = Bd * Tn
    K = idx_list.shape[0]
    idx = jnp.transpose(idx_list)[:R]
    qT = jnp.transpose((q * (HEAD_DIM ** -0.5)).reshape(R, N_HEADS, HEAD_DIM), (0, 2, 1))
    qT = jnp.pad(qT, ((0, 0), (0, 0), (0, LANES - N_HEADS))).astype(bf16)
    rb = jnp.pad(rel_bias.astype(f32), ((0, LANES - NUM_BUCKETS), (0, LANES - N_HEADS)))
    row = lambda r, idx_s, pt_s: (r, 0, 0)
    batch = lambda r, idx_s, pt_s: (r // Tn, 0, 0, 0)
    grid_spec = pltpu.PrefetchScalarGridSpec(
        num_scalar_prefetch=2, grid=(R,),
        in_specs=[pl.BlockSpec((1, HEAD_DIM, LANES), row),
                  pl.BlockSpec((1, K, 1), row),
                  pl.BlockSpec((1, Tn, N_HEADS, HEAD_DIM), batch),
                  pl.BlockSpec((1, Tn, N_HEADS, HEAD_DIM), batch),
                  pl.BlockSpec((LANES, LANES), lambda r, idx_s, pt_s: (0, 0)),
                  pl.BlockSpec(memory_space=pl.ANY),
                  pl.BlockSpec(memory_space=pl.ANY)],
        out_specs=pl.BlockSpec((1, N_HEADS, HEAD_DIM), row),
        scratch_shapes=[pltpu.VMEM((2, K, N_HEADS, HEAD_DIM), f32), pltpu.VMEM((2, K, N_HEADS, HEAD_DIM), f32),
                        pltpu.SemaphoreType.DMA((2,)), pltpu.SemaphoreType.DMA((2,))])
    kern = functools.partial(_sgather_kernel, K=K, tn=Tn, past=past, n_pages=n_pages, ps=ps)
    out = pl.pallas_call(
        kern, grid_spec=grid_spec,
        out_shape=jax.ShapeDtypeStruct((R, N_HEADS, HEAD_DIM), f32),
        compiler_params=pltpu.CompilerParams(dimension_semantics=("arbitrary",), vmem_limit_bytes=VMEM_LIMIT,
                                             disable_bounds_checks=True),
        name="sample_attn",
    )(idx.reshape(R * K), page_table.reshape(Bd * n_pages), qT, idx.reshape(R, K, 1), k_new, v_new, rb,
      cache_k, cache_v)
    return out.reshape(Bd, Tn, N_HEADS * HEAD_DIM)


def _pick(n, cands):
    for c in cands:
        if n % c == 0:
            return c
    return n


def kernel(x_prompt, x_sample, cache_k, cache_v, cache_kidx, state_ssm_re, state_ssm_im, page_table, rel_bias, w_in, ssm_lambda_re, ssm_lambda_im, ssm_log_step, ssm_b_re, ssm_b_im, ssm_c_re, ssm_c_im, ssm_d, w_glu, w_attn_out, w_ssm_out, w_o, ln1_g, ln1_b, w_group_router, b_group_router, w_expert_router, b_expert_router, w_exp_gate, w_exp_up, w_exp_down, ln2_g, ln2_b):
    B, T, D = x_prompt.shape
    Bd, Tn = x_sample.shape[:2]
    n_phys, ps = cache_k.shape[1:3]
    d_attn = N_HEADS * HEAD_DIM
    d_qi = IDX_HEADS * IDX_DIM
    d_ssm = D // 2
    G = d_ssm // SSM_GROUP
    E = N_EXPERT_GROUPS * EXPERTS_PER_GROUP
    l = 0
    w = w_in[l]
    o_q, o_k, o_v, o_qi = 0, d_attn, 2 * d_attn, 3 * d_attn
    o_ki = o_qi + d_qi
    o_wi = o_ki + IDX_DIM
    o_u = o_wi + IDX_HEADS
    o_ga = o_u + d_ssm
    o_gb = o_ga + D
    cols = lambda a, n: w[:, a:a + n]

    small = jnp.concatenate([cols(o_ki, IDX_DIM), cols(o_wi, IDX_HEADS) * INDEXER_SCALE,
                             jnp.zeros((D, LANES - IDX_DIM - IDX_HEADS), f32)], axis=1)
    w_n = jnp.concatenate([cols(o_ga, D), cols(o_gb, D), cols(o_k, d_attn), cols(o_v, d_attn), small, cols(o_u, d_ssm)],
                          axis=1)
    c_ga, c_gb, c_k = 0, D, 2 * D
    c_v = c_k + d_attn
    c_s = c_v + d_attn
    c_u = c_s + LANES
    n_cols = c_u + d_ssm
    tn_n = 768
    w_n = jnp.pad(w_n, ((0, 0), (0, _round_up(n_cols, tn_n) - n_cols))).astype(bf16)
    w_t = jnp.concatenate([cols(o_q, d_attn) * (HEAD_DIM ** -0.5), cols(o_v, d_attn), cols(o_qi, d_qi)], axis=1).T.astype(bf16)
    w_wi = jnp.pad((cols(o_wi, IDX_HEADS) * INDEXER_SCALE).T, ((0, 2 * SUBLANES - IDX_HEADS), (0, 0))).astype(bf16)

    wa, wg, ws, wo = (a[l].astype(bf16) for a in (w_attn_out, w_glu, w_ssm_out, w_o))
    w_r = jnp.concatenate([w_expert_router[l], w_group_router[l], jnp.zeros((D, LANES - E - N_EXPERT_GROUPS), f32)], axis=1)
    b_r = jnp.concatenate([b_expert_router[l], b_group_router[l], jnp.zeros((LANES - E - N_EXPERT_GROUPS,), f32)])[None, :]
    weg, weu, wed = (a[l].astype(bf16) for a in (w_exp_gate, w_exp_up, w_exp_down))
    g1, b1, g2, b2 = (a[l][None, :].astype(f32) for a in (ln1_g, ln1_b, ln2_g, ln2_b))
    ssm_w = (ssm_lambda_re[l], ssm_lambda_im[l], ssm_log_step[l], ssm_b_re[l], ssm_b_im[l], ssm_c_re[l], ssm_c_im[l], ssm_d[l])

    def tail(x2, attn, ssm_y, zn):
        n = x2.shape[0]
        x1 = _merge(x2, attn, ssm_y, zn, c_ga // D, c_gb // D, wa, wg, ws, wo, g1, b1, _pick(n, (256, 128)))
        gates = _router(x1, w_r, b_r, _pick(n, (512, 128)))
        return _moe(x1, gates, weg, weu, wed, g2, b2, _pick(n, (512, 128)))

    N = B * T
    xp2 = x_prompt.reshape(N, D)
    xpb = xp2.astype(bf16)
    tm = _pick(N, (1024, 512, 256, 128))
    zn = _mm_nn(xpb, w_n, tm, tn_n)
    zt = _mm_nt(w_t, xpb, 512, tm, bf16)
    wit = _mm_nt(w_wi, xpb, 2 * SUBLANES, tm, f32)
    k_p, v_p, ki_p = zn[:, c_k:c_k + d_attn], zn[:, c_v:c_v + d_attn], zn[:, c_s:c_s + IDX_DIM]
    tq = _pick(T, (256, 128))
    attn_t = _prompt_attention(zt[:d_attn], zt[d_attn:2 * d_attn], zt[2 * d_attn:], wit,
                               ki_p.reshape(B, T, IDX_DIM).astype(bf16), k_p.astype(bf16),
                               _prompt_bias_tiles(rel_bias, tq), B, T, tq)
    L = _pick(T, (16, 8, 4, 2))
    zeros_state = jnp.zeros((B, G, STATE_DIM), f32)
    y_p, h_p = _ssm(zn, c_u // LANES, d_ssm, B, _pack_state(zeros_state, zeros_state)[:, :, None, :],
                    _ssm_tables(*ssm_w, L), L, True)
    hr_p, hi_p = _unpack_state(h_p[:, :, 0, :], G, STATE_DIM)
    out_p = tail(xp2, attn_t.T, y_p, zn).reshape(B, T, D)

    Ns = Bd * Tn
    xs2 = x_sample.reshape(Ns, D)
    xsb = xs2.astype(bf16)
    zs = _mm_nn(xsb, w_n, Ns, tn_n)
    w_s = jnp.concatenate([cols(o_q, d_attn), cols(o_qi, d_qi)], axis=1)
    w_s = jnp.pad(w_s, ((0, 0), (0, _round_up(d_attn + d_qi, tn_n) - d_attn - d_qi))).astype(bf16)
    zq = _mm_nn(xsb, w_s, Ns, tn_n)
    q_s = zq[:, :d_attn].reshape(Bd, Tn, N_HEADS, HEAD_DIM)
    qi_s = zq[:, d_attn:d_attn + d_qi].reshape(Bd, Tn, IDX_HEADS, IDX_DIM)
    k_s = zs[:, c_k:c_k + d_attn].reshape(Bd, Tn, N_HEADS, HEAD_DIM)
    v_s = zs[:, c_v:c_v + d_attn].reshape(Bd, Tn, N_HEADS, HEAD_DIM)
    ki_s = zs[:, c_s:c_s + IDX_DIM].reshape(Bd, Tn, IDX_DIM)
    wi_s = zs[:, c_s + IDX_DIM:c_s + IDX_DIM + IDX_HEADS].reshape(Bd, Tn, IDX_HEADS)
    n_pages = page_table.shape[1]
    pp = _pick(n_pages, (16, 8, 4, 2, 1))
    sc_past, sc_new = _sample_scores(page_table, qi_s, wi_s, ki_s, cache_kidx[l], pp)
    idx_list = _sample_select(sc_past.reshape(Ns, n_pages * ps), sc_new.reshape(Ns, LANES), Tn)
    attn_s = _sample_attention(page_table, q_s, k_s, v_s, idx_list, rel_bias, cache_k[l], cache_v[l])
    y_s, h_s = _ssm(zs, c_u // LANES, d_ssm, 1, _pack_state(state_ssm_re[l], state_ssm_im[l])[:, None],
                    _ssm_tables(*ssm_w, Tn), Tn, False)
    hr_s, hi_s = _unpack_state(h_s[:, 0], G, STATE_DIM)
    out_s = tail(xs2, attn_s.reshape(Ns, d_attn), y_s, zs).reshape(Bd, Tn, D)

    sdt = state_ssm_re.dtype
    return (out_p, out_s,
            k_p.reshape(1, B, T // ps, ps, N_HEADS, HEAD_DIM), v_p.reshape(1, B, T // ps, ps, N_HEADS, HEAD_DIM),
            ki_p.reshape(1, B, T // ps, ps, IDX_DIM), hr_p.astype(sdt)[None], hi_p.astype(sdt)[None],
            k_s[None], v_s[None], ki_s[None], hr_s.astype(sdt)[None], hi_s.astype(sdt)[None])
```

```python
import functools
import math

import numpy as np
import jax
import jax.numpy as jnp
from jax import lax
from jax.experimental import pallas as pl
from jax.experimental.pallas import tpu as pltpu

f32, bf16, i32 = jnp.float32, jnp.bfloat16, jnp.int32

N_HEADS = 8
HEAD_DIM = 128
IDX_HEADS = 8
IDX_DIM = 64
TOPK_MAX = 256
NUM_BUCKETS = 32
MAX_DISTANCE = 128
SSM_GROUP = 16
STATE_DIM = 64
N_EXPERT_GROUPS = 4
EXPERTS_PER_GROUP = 8
TOP_K_EXPERT = 2
DEPTH = 1
ALPHA = (2 * DEPTH) ** 0.25
LN_EPS = 1e-5
INDEXER_SCALE = (IDX_HEADS ** -0.5) * (IDX_DIM ** -0.5)

LANES = 128
SUBLANES = 8
VMEM_LIMIT = 56 * 1024 * 1024

INT_MIN = np.int32(-2 ** 31)
KEY_NEG_INF = np.int32(-2139095041)
NEG_BIG = -1e30


def _cparams(sem):
    return pltpu.CompilerParams(dimension_semantics=sem, vmem_limit_bytes=VMEM_LIMIT)


def _round_up(a, b):
    return (a + b - 1) // b * b


def _mm_nn_kernel(a_ref, b_ref, o_ref):
    o_ref[...] = jnp.dot(a_ref[...], b_ref[...], preferred_element_type=f32).astype(o_ref.dtype)


def _mm_nn(a, b, tm, tn, out_dtype=f32):
    M, K = a.shape
    N = b.shape[1]
    return pl.pallas_call(
        _mm_nn_kernel, grid=(M // tm, N // tn),
        in_specs=[pl.BlockSpec((tm, K), lambda i, j: (i, 0)), pl.BlockSpec((K, tn), lambda i, j: (0, j))],
        out_specs=pl.BlockSpec((tm, tn), lambda i, j: (i, j)),
        out_shape=jax.ShapeDtypeStruct((M, N), out_dtype),
        compiler_params=_cparams(("parallel", "arbitrary")), name="proj_nn")(a, b)


def _mm_nt_kernel(w_ref, x_ref, o_ref):
    o_ref[...] = lax.dot_general(w_ref[...], x_ref[...], (((1,), (1,)), ((), ())),
                                 preferred_element_type=f32).astype(o_ref.dtype)


def _mm_nt(w, x, tn, tm, out_dtype):
    n, K = w.shape
    M = x.shape[0]
    return pl.pallas_call(
        _mm_nt_kernel, grid=(M // tm, n // tn),
        in_specs=[pl.BlockSpec((tn, K), lambda i, j: (j, 0)), pl.BlockSpec((tm, K), lambda i, j: (i, 0))],
        out_specs=pl.BlockSpec((tn, tm), lambda i, j: (j, i)),
        out_shape=jax.ShapeDtypeStruct((n, M), out_dtype),
        compiler_params=_cparams(("parallel", "arbitrary")), name="proj_nt")(w, x)


def _order_key(x):
    b = lax.bitcast_convert_type(x + 0.0, i32)
    return b ^ ((b >> 31) & np.int32(0x7FFFFFFF))


def _bucket_starts():
    me = NUM_BUCKETS // 2
    d = np.arange(1, 4 * MAX_DISTANCE)
    large = me + (np.log(d.astype(np.float32) / np.float32(me)) / np.float32(math.log(MAX_DISTANCE / me))
                  * np.float32(NUM_BUCKETS - me)).astype(np.int32)
    bucket = np.where(d < me, d, np.minimum(large, NUM_BUCKETS - 1))
    return [int(d[np.argmax(bucket >= b)]) for b in range(me + 1, NUM_BUCKETS)]


def _rel_bucket(dist):
    me = NUM_BUCKETS // 2
    d = jnp.maximum(dist, 0)
    large = me
    for start in _bucket_starts():
        large = large + jnp.where(d >= start, 1, 0)
    return jnp.where(d < me, d, large)


def _select_rows(keys_sc, n_rows, k_top, idx_bits, slab):
    tq = keys_sc.shape[1]
    n_slabs = n_rows // slab

    def count(pred):
        def body(i, cnt):
            r0 = pl.multiple_of(i * slab, slab)
            blk = keys_sc[pl.ds(r0, slab), :]
            hit = jnp.where(pred(blk, r0), 1, 0).astype(i32)
            return cnt + jnp.sum(hit.reshape(slab // SUBLANES, SUBLANES, tq), axis=0)
        cnt = lax.fori_loop(0, n_slabs, body, jnp.zeros((SUBLANES, tq), i32))
        return jnp.sum(cnt, axis=0, keepdims=True)

    def count_ge(cand):
        cb = jnp.broadcast_to(cand, (slab, tq))
        return count(lambda blk, r0: blk >= cb)

    zero = jnp.zeros((1, tq), i32)
    thr = jnp.where(count_ge(zero) >= k_top, zero, jnp.full((1, tq), INT_MIN, i32))

    def bit_body(it, thr):
        cand = thr | lax.shift_left(np.int32(1), np.int32(30) - it)
        return jnp.where(count_ge(cand) >= k_top, cand, thr)

    thr = lax.fori_loop(0, 31, bit_body, thr)
    thr = jnp.maximum(thr, KEY_NEG_INF)
    n_ge = count_ge(thr)

    @pl.when(jnp.max(n_ge) > k_top)
    def _():
        need = k_top - count_ge(thr + 1)
        tb = jnp.broadcast_to(thr, (slab, tq))

        def count_eq_below(m):
            mb = jnp.broadcast_to(m, (slab, tq))
            return count(lambda blk, r0: (blk == tb) & (r0 + lax.broadcasted_iota(i32, (slab, tq), 0) < mb))

        def idx_body(it, m):
            cand = m | lax.shift_left(np.int32(1), np.int32(idx_bits - 1) - it)
            return jnp.where(count_eq_below(cand) < need, cand, m)

        m = lax.fori_loop(0, idx_bits, idx_body, zero)
        mb = jnp.broadcast_to(m, (slab, tq))

        def demote(i, _):
            r0 = pl.multiple_of(i * slab, slab)
            blk = keys_sc[pl.ds(r0, slab), :]
            drop = (blk == tb) & (r0 + lax.broadcasted_iota(i32, (slab, tq), 0) > mb)
            keys_sc[pl.ds(r0, slab), :] = jnp.where(drop, blk - 1, blk)
            return 0

        lax.fori_loop(0, n_slabs, demote, 0)

    return thr


def _pattn_kernel(qiT_ref, wiT_ref, ki_ref, qT_ref, k_ref, vT_ref, bias_ref, oT_ref,
                  keys_sc, thr_sc, msk_sc, m_sc, l_sc, acc_sc, *, k_top, tq, idx_bits):
    qb, kb = _folded_blocks(pl.program_id(1), pl.program_id(2), pl.num_programs(2) - 1)

    @pl.when(kb == 0)
    def _():
        def chunk(c, _):
            r0 = pl.multiple_of(c * tq, tq)
            ki_c = ki_ref[0, pl.ds(r0, tq), :]
            score = jnp.zeros((tq, tq), f32)
            for h in range(IDX_HEADS):
                s = jnp.dot(ki_c, qiT_ref[h * IDX_DIM:(h + 1) * IDX_DIM, :], preferred_element_type=f32)
                score = score + wiT_ref[h:h + 1, :] * jnp.maximum(s, 0.0)
            kpos = r0 + lax.broadcasted_iota(i32, (tq, tq), 0)
            qpos = qb * tq + lax.broadcasted_iota(i32, (tq, tq), 1)
            keys_sc[pl.ds(r0, tq), :] = jnp.where(kpos <= qpos, _order_key(score), INT_MIN)
            return 0

        lax.fori_loop(0, qb + 1, chunk, 0)
        thr_sc[...] = _select_rows(keys_sc, (qb + 1) * tq, k_top, idx_bits, tq)
        m_sc[...] = jnp.full(m_sc.shape, NEG_BIG, f32)
        l_sc[...] = jnp.zeros(l_sc.shape, f32)
        acc_sc[...] = jnp.zeros(acc_sc.shape, f32)

    r0 = pl.multiple_of(kb * tq, tq)
    msk_sc[...] = jnp.where(keys_sc[pl.ds(r0, tq), :] >= thr_sc[...], 0.0, NEG_BIG)
    for h in range(N_HEADS):
        hs = slice(h * HEAD_DIM, (h + 1) * HEAD_DIM)
        for c in range(tq // LANES):
            cs = slice(c * LANES, (c + 1) * LANES)
            s = (jnp.dot(k_ref[:, hs], qT_ref[hs, cs], preferred_element_type=f32)
                 + bias_ref[h, 0, :, cs] + msk_sc[:, cs])
            m_old = m_sc[h:h + 1, cs]
            m_new = jnp.maximum(m_old, jnp.max(s, axis=0, keepdims=True))
            p = jnp.exp(s - m_new)
            alpha = jnp.exp(m_old - m_new)
            l_sc[h:h + 1, cs] = alpha * l_sc[h:h + 1, cs] + jnp.sum(p, axis=0, keepdims=True)
            acc_sc[hs, cs] = alpha * acc_sc[hs, cs] + jnp.dot(vT_ref[hs, :], p.astype(bf16),
                                                              preferred_element_type=f32)
            m_sc[h:h + 1, cs] = m_new

    @pl.when(kb == qb)
    def _():
        for h in range(N_HEADS):
            hs = slice(h * HEAD_DIM, (h + 1) * HEAD_DIM)
            oT_ref[hs, :] = acc_sc[hs, :] / l_sc[h:h + 1, :]


def _folded_blocks(i, j, nq):
    first = j <= i
    return jnp.where(first, i, nq - 1 - i), jnp.where(first, j, j - i - 1)


def _prompt_attention(qT, vT, qiT, wiT, ki, k, bias_tiles, B, T, tq):
    nq = T // tq
    assert nq % 2 == 0
    k_top = min(TOPK_MAX, T // 4)
    d_attn = N_HEADS * HEAD_DIM
    kern = functools.partial(_pattn_kernel, k_top=k_top, tq=tq, idx_bits=max(1, (T - 1).bit_length()))
    qblk = lambda i, j: _folded_blocks(i, j, nq)[0]
    kblk = lambda i, j: _folded_blocks(i, j, nq)[1]
    col = lambda b, i, j: (0, b * nq + qblk(i, j))
    return pl.pallas_call(
        kern, grid=(B, nq // 2, nq + 1),
        in_specs=[
            pl.BlockSpec((IDX_HEADS * IDX_DIM, tq), col),
            pl.BlockSpec((IDX_HEADS, tq), col),
            pl.BlockSpec((1, T, IDX_DIM), lambda b, i, j: (b, 0, 0)),
            pl.BlockSpec((d_attn, tq), col),
            pl.BlockSpec((tq, d_attn), lambda b, i, j: (b * nq + kblk(i, j), 0)),
            pl.BlockSpec((d_attn, tq), lambda b, i, j: (0, b * nq + kblk(i, j))),
            pl.BlockSpec((N_HEADS, 1, tq, tq), lambda b, i, j: (0, jnp.minimum(qblk(i, j) - kblk(i, j), 2), 0, 0)),
        ],
        out_specs=pl.BlockSpec((d_attn, tq), col),
        out_shape=jax.ShapeDtypeStruct((d_attn, B * T), f32),
        scratch_shapes=[pltpu.VMEM((T, tq), i32), pltpu.VMEM((1, tq), i32), pltpu.VMEM((tq, tq), f32),
                        pltpu.VMEM((N_HEADS, tq), f32), pltpu.VMEM((N_HEADS, tq), f32),
                        pltpu.VMEM((d_attn, tq), f32)],
        compiler_params=_cparams(("parallel", "arbitrary", "arbitrary")), name="prompt_attn",
    )(qiT, wiT, ki, qT, k, vT, bias_tiles)


def _bias_lookup_kernel(rb_ref, bkt_ref, o_ref):
    bkt = bkt_ref[0]
    for h in range(N_HEADS):
        acc = jnp.zeros(bkt.shape, f32)
        for b in range(NUM_BUCKETS):
            acc = jnp.where(bkt == b, rb_ref[b, h], acc)
        o_ref[h, 0] = acc


def _prompt_bias_tiles(rel_bias, tq):
    s = jnp.arange(tq, dtype=i32)[:, None]
    t = jnp.arange(tq, dtype=i32)[None, :]
    bkt = jnp.stack([_rel_bucket(d * tq + t - s) for d in range(3)]).astype(i32)
    return pl.pallas_call(
        _bias_lookup_kernel, grid=(3,),
        in_specs=[pl.BlockSpec(memory_space=pltpu.SMEM), pl.BlockSpec((1, tq, tq), lambda d: (d, 0, 0))],
        out_specs=pl.BlockSpec((N_HEADS, 1, tq, tq), lambda d: (0, d, 0, 0)),
        out_shape=jax.ShapeDtypeStruct((N_HEADS, 3, tq, tq), f32),
        compiler_params=_cparams(("parallel",)), name="bias_tiles")(rel_bias.astype(f32), bkt)


def _ssm_tables(lam_re, lam_im, log_step, b_re, b_im, c_re, c_im, d, L):
    hp = lax.Precision.HIGHEST
    G, P = lam_re.shape
    H = b_re.shape[-1]
    lr, li = lam_re.astype(f32), lam_im.astype(f32)
    step = jnp.exp(log_step.astype(f32))[:, None]
    taus = jnp.arange(L + 1, dtype=f32)[:, None, None]
    mag = jnp.exp(lr * step * taus)
    pr, pi = mag * jnp.cos(li * step * taus), mag * jnp.sin(li * step * taus)
    x, y, den = pr[1] - 1.0, pi[1], lr * lr + li * li
    fr, fi = (x * lr + y * li) / den, (y * lr - x * li) / den
    bbr = fr[..., None] * b_re - fi[..., None] * b_im
    bbi = fr[..., None] * b_im + fi[..., None] * b_re
    clr = c_re[None] * pr[:, :, None, :] - c_im[None] * pi[:, :, None, :]
    cli = c_re[None] * pi[:, :, None, :] + c_im[None] * pr[:, :, None, :]
    kern = (jnp.einsum('tghp,gpk->tghk', clr[:L], bbr, precision=hp)
            - jnp.einsum('tghp,gpk->tghk', cli[:L], bbi, precision=hp))
    kern = kern.at[0].add(jnp.eye(H, dtype=f32)[None] * d.astype(f32)[:, :, None])
    GB = LANES // H
    J = G // GB
    rev = jnp.arange(L - 1, -1, -1)
    wr = pr[rev][:, :, :, None] * bbr[None] - pi[rev][:, :, :, None] * bbi[None]
    wi = pr[rev][:, :, :, None] * bbi[None] + pi[rev][:, :, :, None] * bbr[None]
    ka = jnp.transpose(kern.reshape(L, J, GB, H, H), (1, 0, 2, 4, 3)).reshape(J, L, LANES, H)
    to_wa = lambda a: jnp.transpose(a.reshape(L, J, GB, P, H), (1, 0, 2, 4, 3)).reshape(J, L, LANES, P)
    wa = jnp.stack([to_wa(wr), to_wa(wi)], axis=1)
    to_va = lambda a: jnp.transpose(a.reshape(L, J, GB, H, P), (1, 0, 2, 4, 3)).reshape(J, L, GB * P, H)
    va = jnp.stack([to_va(clr[1:]), -to_va(cli[1:])], axis=1)
    mT, w, v = pl.pallas_call(
        functools.partial(_ssm_expand_kernel, L=L), grid=(J,),
        in_specs=[pl.BlockSpec((1, L, LANES, H), lambda j: (j, 0, 0, 0)),
                  pl.BlockSpec((1, 2, L, LANES, P), lambda j: (j, 0, 0, 0, 0)),
                  pl.BlockSpec((1, 2, L, GB * P, H), lambda j: (j, 0, 0, 0, 0))],
        out_specs=[pl.BlockSpec((1, L * LANES, L * LANES), lambda j: (j, 0, 0)),
                   pl.BlockSpec((1, L * LANES, 2 * GB * P), lambda j: (j, 0, 0)),
                   pl.BlockSpec((1, 2 * GB * P, L * LANES), lambda j: (j, 0, 0))],
        out_shape=[jax.ShapeDtypeStruct((J, L * LANES, L * LANES), bf16),
                   jax.ShapeDtypeStruct((J, L * LANES, 2 * GB * P), bf16),
                   jax.ShapeDtypeStruct((J, 2 * GB * P, L * LANES), bf16)],
        compiler_params=_cparams(("parallel",)), name="ssm_tables")(ka, wa, va)
    lam = jnp.concatenate([pr[L].reshape(J, 1, GB * P), pi[L].reshape(J, 1, GB * P)], axis=2)
    return mT, w, v, lam


def _ssm_expand_kernel(ka_ref, wa_ref, va_ref, mt_ref, w_ref, v_ref, *, L):
    H = ka_ref.shape[3]
    P = wa_ref.shape[4]
    S = v_ref.shape[1] // 2

    def repeat(n, width):
        return jnp.where(lax.broadcasted_iota(i32, (n, width), 1) % n == lax.broadcasted_iota(i32, (n, width), 0),
                         1.0, 0.0).astype(bf16)

    def same_group(rows, rdiv, cols, cdiv):
        return (lax.broadcasted_iota(i32, (rows, cols), 0) // rdiv) == (lax.broadcasted_iota(i32, (rows, cols), 1) // cdiv)

    rep_h, rep_p = repeat(H, LANES), repeat(P, S)
    m_mask, w_mask, v_mask = same_group(LANES, H, LANES, H), same_group(LANES, H, S, P), same_group(S, P, LANES, H)
    zero = jnp.zeros((LANES, LANES), bf16)
    for tau in range(L):
        tile = jnp.where(m_mask, jnp.dot(ka_ref[0, tau].astype(bf16), rep_h, preferred_element_type=f32), 0.0).astype(bf16)
        for s in range(L - tau):
            mt_ref[0, s * LANES:(s + 1) * LANES, (s + tau) * LANES:(s + tau + 1) * LANES] = tile
    for s in range(L):
        for t in range(s):
            mt_ref[0, s * LANES:(s + 1) * LANES, t * LANES:(t + 1) * LANES] = zero
    for half in range(2):
        for s in range(L):
            wt = jnp.dot(wa_ref[0, half, s].astype(bf16), rep_p, preferred_element_type=f32)
            w_ref[0, s * LANES:(s + 1) * LANES, half * S:(half + 1) * S] = jnp.where(w_mask, wt, 0.0).astype(bf16)
            vt = jnp.dot(va_ref[0, half, s].astype(bf16), rep_h, preferred_element_type=f32)
            v_ref[0, half * S:(half + 1) * S, s * LANES:(s + 1) * LANES] = jnp.where(v_mask, vt, 0.0).astype(bf16)


def _ssm_kernel(u_ref, mt_ref, w_ref, v_ref, lam_ref, h0_ref, y_ref, hout_ref, uc_sc, s_sc, hin_sc, yc_sc,
                *, L, rows, seq):
    S = s_sc.shape[1] // 2
    for s in range(L):
        uc_sc[:, s * LANES:(s + 1) * LANES] = u_ref[pl.ds(s, rows, stride=L), :].astype(bf16)
    uc = uc_sc[...]
    s_sc[...] = jnp.dot(uc, w_ref[0], preferred_element_type=f32)
    lr, li = lam_ref[0, :, 0:S], lam_ref[0, :, S:2 * S]
    if seq:
        def body(c, carry):
            hr, hi = carry
            hin_sc[pl.ds(c, 1), 0:S] = hr
            hin_sc[pl.ds(c, 1), S:2 * S] = hi
            sr, si = s_sc[pl.ds(c, 1), 0:S], s_sc[pl.ds(c, 1), S:2 * S]
            return lr * hr - li * hi + sr, lr * hi + li * hr + si

        hr, hi = lax.fori_loop(0, rows, body, (h0_ref[0, 0, :, 0:S], h0_ref[0, 0, :, S:2 * S]))
    else:
        hr0, hi0 = h0_ref[0, 0, :, 0:S], h0_ref[0, 0, :, S:2 * S]
        hin_sc[...] = h0_ref[0, 0]
        hr = lr * hr0 - li * hi0 + s_sc[:, 0:S]
        hi = lr * hi0 + li * hr0 + s_sc[:, S:2 * S]
    hout_ref[0, 0, :, 0:S] = hr
    hout_ref[0, 0, :, S:2 * S] = hi
    yc_sc[...] = (jnp.dot(uc, mt_ref[0], preferred_element_type=f32)
                  + jnp.dot(hin_sc[...].astype(bf16), v_ref[0], preferred_element_type=f32))
    for t in range(L):
        y_ref[pl.ds(t, rows, stride=L), :] = yc_sc[:, t * LANES:(t + 1) * LANES]


def _ssm(zn, u_blk0, d_ssm, n_steps, h0, tables, L, seq):
    mT, w, v, lam = tables
    N = zn.shape[0]
    J = d_ssm // LANES
    tok = N // n_steps
    rows = tok // L
    rows_h = h0.shape[2]
    S2 = h0.shape[3]
    kern = functools.partial(_ssm_kernel, L=L, rows=rows, seq=seq)
    return pl.pallas_call(
        kern, grid=(J, n_steps),
        in_specs=[
            pl.BlockSpec((tok, LANES), lambda j, b: (b, u_blk0 + j)),
            pl.BlockSpec((1, L * LANES, L * LANES), lambda j, b: (j, 0, 0)),
            pl.BlockSpec((1, L * LANES, S2), lambda j, b: (j, 0, 0)),
            pl.BlockSpec((1, S2, L * LANES), lambda j, b: (j, 0, 0)),
            pl.BlockSpec((1, 1, S2), lambda j, b: (j, 0, 0)),
            pl.BlockSpec((1, 1, rows_h, S2), lambda j, b: (j, b, 0, 0)),
        ],
        out_specs=[pl.BlockSpec((tok, LANES), lambda j, b: (b, j)),
                   pl.BlockSpec((1, 1, rows_h, S2), lambda j, b: (j, b, 0, 0))],
        out_shape=[jax.ShapeDtypeStruct((N, d_ssm), f32),
                   jax.ShapeDtypeStruct(h0.shape, f32)],
        scratch_shapes=[pltpu.VMEM((rows, L * LANES), bf16), pltpu.VMEM((rows, S2), f32),
                        pltpu.VMEM((rows, S2), f32), pltpu.VMEM((rows, L * LANES), f32)],
        compiler_params=_cparams(("parallel", "arbitrary")), name="ssm",
    )(zn, mT, w, v, lam, h0)


def _pack_state(a_re, a_im):
    Bn, G, P = a_re.shape
    J = G * SSM_GROUP // LANES
    blk = lambda a: jnp.transpose(a.astype(f32).reshape(Bn, J, (G // J) * P), (1, 0, 2))
    return jnp.concatenate([blk(a_re), blk(a_im)], axis=2)


def _unpack_state(h, G, P):
    J, Bn, S2 = h.shape
    un = lambda a: jnp.transpose(a, (1, 0, 2)).reshape(Bn, G, P)
    return un(h[:, :, :S2 // 2]), un(h[:, :, S2 // 2:])


def _sigmoid(x):
    return 1.0 / (1.0 + jnp.exp(-x))


def _gelu_tanh(x):
    return 0.5 * x * (1.0 + jnp.tanh(math.sqrt(2.0 / math.pi) * (x + 0.044715 * (x * x * x))))


def _layer_norm(r, g, b):
    mu = jnp.mean(r, axis=-1, keepdims=True)
    c = r - mu
    var = jnp.mean(c * c, axis=-1, keepdims=True)
    return c * lax.rsqrt(var + LN_EPS) * g + b


def _merge_kernel(x_ref, attn_ref, y_ref, ga_ref, gb_ref, wa_ref, wg_ref, ws_ref, wo_ref, g_ref, b_ref, o_ref):
    a = jnp.dot(attn_ref[...].astype(bf16), wa_ref[...], preferred_element_type=f32)
    z = _gelu_tanh(y_ref[...])
    gl = jnp.dot(z.astype(bf16), wg_ref[...], preferred_element_type=f32)
    s = jnp.dot((z * _sigmoid(gl)).astype(bf16), ws_ref[...], preferred_element_type=f32)
    m = _sigmoid(ga_ref[...]) * a + _sigmoid(gb_ref[...]) * s
    mix = jnp.dot(m.astype(bf16), wo_ref[...], preferred_element_type=f32)
    o_ref[...] = _layer_norm(ALPHA * x_ref[...] + mix, g_ref[...], b_ref[...])


def _merge(x, attn, ssm_y, zn, ga_blk, gb_blk, wa, wg, ws, wo, g, b, tm):
    N, D = x.shape
    Dh = attn.shape[1]
    const = lambda shape: pl.BlockSpec(shape, lambda i: (0, 0), pipeline_mode=pl.Buffered(1))
    return pl.pallas_call(
        _merge_kernel, grid=(N // tm,),
        in_specs=[pl.BlockSpec((tm, D), lambda i: (i, 0)),
                  pl.BlockSpec((tm, Dh), lambda i: (i, 0)),
                  pl.BlockSpec((tm, Dh), lambda i: (i, 0)),
                  pl.BlockSpec((tm, D), lambda i: (i, ga_blk)),
                  pl.BlockSpec((tm, D), lambda i: (i, gb_blk)),
                  const((Dh, D)), const((Dh, Dh)), const((Dh, D)), const((D, D)),
                  const((1, D)), const((1, D))],
        out_specs=pl.BlockSpec((tm, D), lambda i: (i, 0)),
        out_shape=jax.ShapeDtypeStruct((N, D), f32),
        compiler_params=_cparams(("parallel",)), name="merge_ln1",
    )(x, attn, ssm_y, zn, zn, wa, wg, ws, wo, g, b)


def _router_kernel(x_ref, w_ref, b_ref, o_ref):
    E = N_EXPERT_GROUPS * EXPERTS_PER_GROUP
    logit = jnp.dot(x_ref[...], w_ref[...], preferred_element_type=f32, precision=lax.Precision.HIGHEST) + b_ref[...]
    lane = lax.broadcasted_iota(i32, logit.shape, 1)
    is_g = (lane >= E) & (lane < E + N_EXPERT_GROUPS)
    glog = jnp.where(is_g, logit, -jnp.inf)
    g_max = jnp.max(glog, axis=1, keepdims=True)
    g_idx = jnp.min(jnp.where(glog == g_max, lane, 4 * LANES), axis=1, keepdims=True) - E
    g_w = 1.0 / jnp.sum(jnp.where(is_g, jnp.exp(glog - g_max), 0.0), axis=1, keepdims=True)
    in_g = (lane < E) & (lane // EXPERTS_PER_GROUP == g_idx)
    e1 = jnp.where(in_g, logit, -jnp.inf)
    v1 = jnp.max(e1, axis=1, keepdims=True)
    i1 = jnp.min(jnp.where(e1 == v1, lane, 4 * LANES), axis=1, keepdims=True)
    e2 = jnp.where(lane == i1, -jnp.inf, e1)
    v2 = jnp.max(e2, axis=1, keepdims=True)
    i2 = jnp.min(jnp.where(e2 == v2, lane, 4 * LANES), axis=1, keepdims=True)
    t = jnp.exp(v2 - v1)
    w1 = g_w / (1.0 + t)
    w2 = g_w * t / (1.0 + t)
    o_ref[...] = jnp.where(lane == i1, w1, 0.0) + jnp.where(lane == i2, w2, 0.0)


def _router(x1, w_r, b_r, tm):
    N, D = x1.shape
    return pl.pallas_call(
        _router_kernel, grid=(N // tm,),
        in_specs=[pl.BlockSpec((tm, D), lambda i: (i, 0)),
                  pl.BlockSpec((D, LANES), lambda i: (0, 0)),
                  pl.BlockSpec((1, LANES), lambda i: (0, 0))],
        out_specs=pl.BlockSpec((tm, LANES), lambda i: (i, 0)),
        out_shape=jax.ShapeDtypeStruct((N, LANES), f32),
        compiler_params=_cparams(("parallel",)), name="router",
    )(x1, w_r, b_r)


def _moe_kernel(x_ref, gate_ref, wg_ref, wu_ref, wd_ref, g_ref, b_ref, o_ref, xb_sc, acc_sc):
    e = pl.program_id(1)

    @pl.when(e == 0)
    def _():
        xb_sc[...] = x_ref[...].astype(bf16)
        acc_sc[...] = jnp.zeros(acc_sc.shape, f32)

    gates = gate_ref[...]
    lane = lax.broadcasted_iota(i32, gates.shape, 1)
    ge = jnp.sum(jnp.where(lane == e, gates, 0.0), axis=1, keepdims=True)
    xb = xb_sc[...]
    hg = jnp.dot(xb, wg_ref[0], preferred_element_type=f32)
    hu = jnp.dot(xb, wu_ref[0], preferred_element_type=f32)
    h = hg * _sigmoid(hg) * hu
    acc_sc[...] += jnp.dot((h * ge).astype(bf16), wd_ref[0], preferred_element_type=f32)

    @pl.when(e == pl.num_programs(1) - 1)
    def _():
        o_ref[...] = _layer_norm(ALPHA * x_ref[...] + acc_sc[...], g_ref[...], b_ref[...])


def _moe(x1, gates, wg, wu, wd, g, b, tm):
    N, D = x1.shape
    E, _, De = wg.shape
    return pl.pallas_call(
        _moe_kernel, grid=(N // tm, E),
        in_specs=[pl.BlockSpec((tm, D), lambda i, e: (i, 0)),
                  pl.BlockSpec((tm, LANES), lambda i, e: (i, 0)),
                  pl.BlockSpec((1, D, De), lambda i, e: (e, 0, 0)),
                  pl.BlockSpec((1, D, De), lambda i, e: (e, 0, 0)),
                  pl.BlockSpec((1, De, D), lambda i, e: (e, 0, 0)),
                  pl.BlockSpec((1, D), lambda i, e: (0, 0)),
                  pl.BlockSpec((1, D), lambda i, e: (0, 0))],
        out_specs=pl.BlockSpec((tm, D), lambda i, e: (i, 0)),
        out_shape=jax.ShapeDtypeStruct((N, D), f32),
        scratch_shapes=[pltpu.VMEM((tm, D), bf16), pltpu.VMEM((tm, D), f32)],
        compiler_params=_cparams(("parallel", "arbitrary")), name="moe_ln2",
    )(x1, gates, wg, wu, wd, g, b)


def _sscore_kernel(pt_ref, qi_ref, wi_ref, kn_ref, *rest, pp):
    pages, (o_ref, on_ref) = rest[:pp], rest[pp:]
    qi = qi_ref[0]
    wi = wi_ref[0]
    tn = qi.shape[0] // IDX_HEADS

    def scores(keys):
        s = lax.dot_general(qi, keys, (((1,), (1,)), ((), ())), preferred_element_type=f32)
        r = jnp.maximum(s, 0.0) * wi
        return jnp.sum(r.reshape(tn, IDX_HEADS, keys.shape[0]), axis=1)

    for j in range(pp):
        o_ref[0, :, j * LANES:(j + 1) * LANES] = scores(pages[j][0].astype(bf16))

    @pl.when(pl.program_id(1) == 0)
    def _():
        on_ref[0] = scores(kn_ref[0])


def _sample_scores(page_table, qi, wi, ki_new, cache_kidx, pp):
    Bd, Tn = qi.shape[:2]
    n_pages = page_table.shape[1]
    ps = cache_kidx.shape[1]
    rows = Tn * IDX_HEADS
    qi2 = qi.reshape(Bd, rows, IDX_DIM).astype(bf16)
    wi2 = wi.reshape(Bd, rows, 1)
    kn = jnp.pad(ki_new, ((0, 0), (0, LANES - Tn), (0, 0))).astype(bf16)
    page_spec = lambda j: pl.BlockSpec((1, ps, IDX_DIM), lambda b, p, pt: (pt[b, p * pp + j], 0, 0))
    grid_spec = pltpu.PrefetchScalarGridSpec(
        num_scalar_prefetch=1, grid=(Bd, n_pages // pp),
        in_specs=[pl.BlockSpec((1, rows, IDX_DIM), lambda b, p, pt: (b, 0, 0)),
                  pl.BlockSpec((1, rows, 1), lambda b, p, pt: (b, 0, 0)),
                  pl.BlockSpec((1, LANES, IDX_DIM), lambda b, p, pt: (b, 0, 0))]
                 + [page_spec(j) for j in range(pp)],
        out_specs=[pl.BlockSpec((1, Tn, pp * ps), lambda b, p, pt: (b, 0, p)),
                   pl.BlockSpec((1, Tn, LANES), lambda b, p, pt: (b, 0, 0))])
    return pl.pallas_call(
        functools.partial(_sscore_kernel, pp=pp), grid_spec=grid_spec,
        out_shape=[jax.ShapeDtypeStruct((Bd, Tn, n_pages * ps), f32),
                   jax.ShapeDtypeStruct((Bd, Tn, LANES), f32)],
        compiler_params=_cparams(("parallel", "arbitrary")), name="sample_scores",
    )(page_table, qi2, wi2, kn, *([cache_kidx] * pp))


def _sselect_kernel(sp_ref, sn_ref, o_ref, keys_sc, *, k_top, tn, idx_bits):
    rows = sp_ref.shape[0]
    past = sp_ref.shape[1]
    n_tiles = past // LANES

    def load(j, _):
        c0 = pl.multiple_of(j * LANES, LANES)
        keys_sc[pl.ds(c0, LANES), :] = _order_key(jnp.transpose(sp_ref[:, pl.ds(c0, LANES)]))
        return 0

    lax.fori_loop(0, n_tiles, load, 0)
    new = _order_key(jnp.transpose(sn_ref[...]))
    jj = lax.broadcasted_iota(i32, new.shape, 0)
    qq = lax.broadcasted_iota(i32, new.shape, 1) % tn
    keys_sc[pl.ds(past, LANES), :] = jnp.where(jj <= qq, new, INT_MIN)
    thr = _select_rows(keys_sc, past + LANES, k_top, idx_bits, LANES)

    none = np.int32(2 ** 30)
    o_ref[...] = jnp.zeros(o_ref.shape, i32)
    slot = lax.broadcasted_iota(i32, o_ref.shape, 0)

    def slab(j, filled):
        c0 = pl.multiple_of(j * LANES, LANES)
        sel = keys_sc[pl.ds(c0, LANES), :] >= thr
        cand = jnp.where(sel, c0 + lax.broadcasted_iota(i32, sel.shape, 0), none)
        n_sel = jnp.sum(jnp.where(sel, 1, 0).astype(i32), axis=0, keepdims=True)

        def extract(i, cand):
            first = jnp.min(cand, axis=0, keepdims=True)
            o_ref[...] = jnp.where((slot == filled + i) & (first < none), first, o_ref[...])
            return jnp.where(cand == first, none, cand)

        lax.fori_loop(0, jnp.max(n_sel), extract, cand)
        return filled + n_sel

    lax.fori_loop(0, n_tiles + 1, slab, jnp.zeros((1, rows), i32))


def _sample_select(scores_past, scores_new, tn):
    assert scores_past.shape[0] <= LANES
    pad_rows = lambda a: jnp.pad(a, ((0, LANES - a.shape[0]), (0, 0)))
    scores_past, scores_new = pad_rows(scores_past), pad_rows(scores_new)
    R, past = scores_past.shape
    k_top = min(TOPK_MAX, (past + tn) // 4)
    assert past + 1 >= k_top
    kern = functools.partial(_sselect_kernel, k_top=k_top, tn=tn, idx_bits=(past + LANES - 1).bit_length())
    return pl.pallas_call(
        kern, grid=(1,),
        in_specs=[pl.BlockSpec((R, past), lambda i: (0, 0)), pl.BlockSpec((R, LANES), lambda i: (0, 0))],
        out_specs=pl.BlockSpec((k_top, R), lambda i: (0, 0)),
        out_shape=jax.ShapeDtypeStruct((k_top, R), i32),
        scratch_shapes=[pltpu.VMEM((past + LANES, R), i32)],
        compiler_params=_cparams(("arbitrary",)), name="sample_select",
    )(scores_past, scores_new)


def _sgather_kernel(idx_ref, pt_ref, qT_ref, idxv_ref, knew_ref, vnew_ref, rb_ref, ck_hbm, cv_hbm, o_ref,
                    kbuf, vbuf, ksem, vsem, *, K, tn, past, n_pages, ps):
    r = pl.program_id(0)
    n = pl.num_programs(0)
    slot = r % 2

    def row_copies(rr, sl, j):
        idx = jnp.minimum(idx_ref[rr * K + j], past - 1)
        phys = pt_ref[(rr // tn) * n_pages + idx // ps]
        return (pltpu.make_async_copy(ck_hbm.at[phys, idx % ps], kbuf.at[sl, j], ksem.at[sl]),
                pltpu.make_async_copy(cv_hbm.at[phys, idx % ps], vbuf.at[sl, j], vsem.at[sl]))

    def issue(rr, sl):
        def body(j, _):
            ck, cv = row_copies(rr, sl, j)
            ck.start()
            cv.start()
            return 0
        lax.fori_loop(0, K, body, 0)

    @pl.when(r == 0)
    def _():
        issue(0, 0)

    @pl.when(r + 1 < n)
    def _():
        issue(r + 1, 1 - slot)

    def wait_body(j, _):
        ck, cv = row_copies(r, slot, j)
        ck.wait()
        cv.wait()
        return 0

    lax.fori_loop(0, K, wait_body, 0)

    for t in range(tn):
        j = K - 1 - t
        pos = idx_ref[r * K + j]

        @pl.when(pos >= past)
        def _():
            kbuf[slot, j] = knew_ref[0, pos - past]
            vbuf[slot, j] = vnew_ref[0, pos - past]

    idxv = idxv_ref[0]
    k = kbuf[slot]
    v = vbuf[slot]
    pairs = jnp.dot(k.reshape(K * N_HEADS, HEAD_DIM).astype(bf16), qT_ref[0], preferred_element_type=f32)
    pairs = pairs.reshape(K, N_HEADS, LANES)
    diag = lax.broadcasted_iota(i32, pairs.shape, 1) == lax.broadcasted_iota(i32, pairs.shape, 2)
    logits = jnp.sum(jnp.where(diag, pairs, 0.0), axis=1)
    dist = past + r % tn - idxv
    onehot = jnp.where(_rel_bucket(dist) == lax.broadcasted_iota(i32, (K, LANES), 1), 1.0, 0.0)
    logits = logits + jnp.dot(onehot, rb_ref[...], preferred_element_type=f32, precision=lax.Precision.HIGHEST)
    p = jnp.exp(logits - jnp.max(logits, axis=0, keepdims=True))
    p = p / jnp.sum(p, axis=0, keepdims=True)
    pd = jnp.where(diag, jnp.broadcast_to(p[:, None, :], pairs.shape), 0.0).reshape(K * N_HEADS, LANES)
    spread = jnp.dot(pd.astype(bf16), jnp.ones((LANES, HEAD_DIM), bf16), preferred_element_type=f32)
    o_ref[0] = jnp.sum(spread.reshape(K, N_HEADS, HEAD_DIM) * v, axis=0)


def _sample_attention(page_table, q, k_new, v_new, idx_list, rel_bias, cache_k, cache_v):
    Bd, Tn = q.shape[:2]
    n_pages = page_table.shape[1]
    ps = cache_k.shape[1]
    past = n_pages * ps
    R = Bd * Tn
    K = idx_list.shape[0]
    idx = jnp.transpose(idx_list)[:R]
    qT = jnp.transpose((q * (HEAD_DIM ** -0.5)).reshape(R, N_HEADS, HEAD_DIM), (0, 2, 1))
    qT = jnp.pad(qT, ((0, 0), (0, 0), (0, LANES - N_HEADS))).astype(bf16)
    rb = jnp.pad(rel_bias.astype(f32), ((0, LANES - NUM_BUCKETS), (0, LANES - N_HEADS)))
    row = lambda r, idx_s, pt_s: (r, 0, 0)
    batch = lambda r, idx_s, pt_s: (r // Tn, 0, 0, 0)
    grid_spec = pltpu.PrefetchScalarGridSpec(
        num_scalar_prefetch=2, grid=(R,),
        in_specs=[pl.BlockSpec((1, HEAD_DIM, LANES), row),
                  pl.BlockSpec((1, K, 1), row),
                  pl.BlockSpec((1, Tn, N_HEADS, HEAD_DIM), batch),
                  pl.BlockSpec((1, Tn, N_HEADS, HEAD_DIM), batch),
                  pl.BlockSpec((LANES, LANES), lambda r, idx_s, pt_s: (0, 0)),
                  pl.BlockSpec(memory_space=pl.ANY),
                  pl.BlockSpec(memory_space=pl.ANY)],
        out_specs=pl.BlockSpec((1, N_HEADS, HEAD_DIM), row),
        scratch_shapes=[pltpu.VMEM((2, K, N_HEADS, HEAD_DIM), f32), pltpu.VMEM((2, K, N_HEADS, HEAD_DIM), f32),
                        pltpu.SemaphoreType.DMA((2,)), pltpu.SemaphoreType.DMA((2,))])
    kern = functools.partial(_sgather_kernel, K=K, tn=Tn, past=past, n_pages=n_pages, ps=ps)
    out = pl.pallas_call(
        kern, grid_spec=grid_spec,
        out_shape=jax.ShapeDtypeStruct((R, N_HEADS, HEAD_DIM), f32),
        compiler_params=pltpu.CompilerParams(dimension_semantics=("arbitrary",), vmem_limit_bytes=VMEM_LIMIT,
                                             disable_bounds_checks=True),
        name="sample_attn",
    )(idx.reshape(R * K), page_table.reshape(Bd * n_pages), qT, idx.reshape(R, K, 1), k_new, v_new, rb,
      cache_k, cache_v)
    return out.reshape(Bd, Tn, N_HEADS * HEAD_DIM)


def _pick(n, cands):
    for c in cands:
        if n % c == 0:
            return c
    return n


def kernel(x_prompt, x_sample, cache_k, cache_v, cache_kidx, state_ssm_re, state_ssm_im, page_table, rel_bias, w_in, ssm_lambda_re, ssm_lambda_im, ssm_log_step, ssm_b_re, ssm_b_im, ssm_c_re, ssm_c_im, ssm_d, w_glu, w_attn_out, w_ssm_out, w_o, ln1_g, ln1_b, w_group_router, b_group_router, w_expert_router, b_expert_router, w_exp_gate, w_exp_up, w_exp_down, ln2_g, ln2_b):
    B, T, D = x_prompt.shape
    Bd, Tn = x_sample.shape[:2]
    n_phys, ps = cache_k.shape[1:3]
    d_attn = N_HEADS * HEAD_DIM
    d_qi = IDX_HEADS * IDX_DIM
    d_ssm = D // 2
    G = d_ssm // SSM_GROUP
    E = N_EXPERT_GROUPS * EXPERTS_PER_GROUP
    l = 0
    w = w_in[l]
    o_q, o_k, o_v, o_qi = 0, d_attn, 2 * d_attn, 3 * d_attn
    o_ki = o_qi + d_qi
    o_wi = o_ki + IDX_DIM
    o_u = o_wi + IDX_HEADS
    o_ga = o_u + d_ssm
    o_gb = o_ga + D
    cols = lambda a, n: w[:, a:a + n]

    small = jnp.concatenate([cols(o_ki, IDX_DIM), cols(o_wi, IDX_HEADS) * INDEXER_SCALE,
                             jnp.zeros((D, LANES - IDX_DIM - IDX_HEADS), f32)], axis=1)
    w_n = jnp.concatenate([cols(o_ga, D), cols(o_gb, D), cols(o_k, d_attn), cols(o_v, d_attn), small, cols(o_u, d_ssm)],
                          axis=1)
    c_ga, c_gb, c_k = 0, D, 2 * D
    c_v = c_k + d_attn
    c_s = c_v + d_attn
    c_u = c_s + LANES
    n_cols = c_u + d_ssm
    tn_n = 768
    w_n = jnp.pad(w_n, ((0, 0), (0, _round_up(n_cols, tn_n) - n_cols))).astype(bf16)
    w_t = jnp.concatenate([cols(o_q, d_attn) * (HEAD_DIM ** -0.5), cols(o_v, d_attn), cols(o_qi, d_qi)], axis=1).T.astype(bf16)
    w_wi = jnp.pad((cols(o_wi, IDX_HEADS) * INDEXER_SCALE).T, ((0, 2 * SUBLANES - IDX_HEADS), (0, 0))).astype(bf16)

    wa, wg, ws, wo = (a[l].astype(bf16) for a in (w_attn_out, w_glu, w_ssm_out, w_o))
    w_r = jnp.concatenate([w_expert_router[l], w_group_router[l], jnp.zeros((D, LANES - E - N_EXPERT_GROUPS), f32)], axis=1)
    b_r = jnp.concatenate([b_expert_router[l], b_group_router[l], jnp.zeros((LANES - E - N_EXPERT_GROUPS,), f32)])[None, :]
    weg, weu, wed = (a[l].astype(bf16) for a in (w_exp_gate, w_exp_up, w_exp_down))
    g1, b1, g2, b2 = (a[l][None, :].astype(f32) for a in (ln1_g, ln1_b, ln2_g, ln2_b))
    ssm_w = (ssm_lambda_re[l], ssm_lambda_im[l], ssm_log_step[l], ssm_b_re[l], ssm_b_im[l], ssm_c_re[l], ssm_c_im[l], ssm_d[l])

    def tail(x2, attn, ssm_y, zn):
        n = x2.shape[0]
        x1 = _merge(x2, attn, ssm_y, zn, c_ga // D, c_gb // D, wa, wg, ws, wo, g1, b1, _pick(n, (256, 128)))
        gates = _router(x1, w_r, b_r, _pick(n, (512, 128)))
        return _moe(x1, gates, weg, weu, wed, g2, b2, _pick(n, (512, 128)))

    N = B * T
    xp2 = x_prompt.reshape(N, D)
    xpb = xp2.astype(bf16)
    tm = _pick(N, (1024, 512, 256, 128))
    zn = _mm_nn(xpb, w_n, tm, tn_n)
    zt = _mm_nt(w_t, xpb, 512, tm, bf16)
    wit = _mm_nt(w_wi, xpb, 2 * SUBLANES, tm, f32)
    k_p, v_p, ki_p = zn[:, c_k:c_k + d_attn], zn[:, c_v:c_v + d_attn], zn[:, c_s:c_s + IDX_DIM]
    tq = _pick(T, (256, 128))
    attn_t = _prompt_attention(zt[:d_attn], zt[d_attn:2 * d_attn], zt[2 * d_attn:], wit,
                               ki_p.reshape(B, T, IDX_DIM).astype(bf16), k_p.astype(bf16),
                               _prompt_bias_tiles(rel_bias, tq), B, T, tq)
    L = _pick(T, (16, 8, 4, 2))
    zeros_state = jnp.zeros((B, G, STATE_DIM), f32)
    y_p, h_p = _ssm(zn, c_u // LANES, d_ssm, B, _pack_state(zeros_state, zeros_state)[:, :, None, :],
                    _ssm_tables(*ssm_w, L), L, True)
    hr_p, hi_p = _unpack_state(h_p[:, :, 0, :], G, STATE_DIM)
    out_p = tail(xp2, attn_t.T, y_p, zn).reshape(B, T, D)

    Ns = Bd * Tn
    xs2 = x_sample.reshape(Ns, D)
    xsb = xs2.astype(bf16)
    zs = _mm_nn(xsb, w_n, Ns, tn_n)
    w_s = jnp.concatenate([cols(o_q, d_attn), cols(o_qi, d_qi)], axis=1)
    w_s = jnp.pad(w_s, ((0, 0), (0, _round_up(d_attn + d_qi, tn_n) - d_attn - d_qi))).astype(bf16)
    zq = _mm_nn(xsb, w_s, Ns, tn_n)
    q_s = zq[:, :d_attn].reshape(Bd, Tn, N_HEADS, HEAD_DIM)
    qi_s = zq[:, d_attn:d_attn + d_qi].reshape(Bd, Tn, IDX_HEADS, IDX_DIM)
    k_s = zs[:, c_k:c_k + d_attn].reshape(Bd, Tn, N_HEADS, HEAD_DIM)
    v_s = zs[:, c_v:c_v + d_attn].reshape(Bd, Tn, N_HEADS, HEAD_DIM)
    ki_s = zs[:, c_s:c_s + IDX_DIM].reshape(Bd, Tn, IDX_DIM)
    wi_s = zs[:, c_s + IDX_DIM:c_s + IDX_DIM + IDX_HEADS].reshape(Bd, Tn, IDX_HEADS)
    n_pages = page_table.shape[1]
    pp = _pick(n_pages, (16, 8, 4, 2, 1))
    sc_past, sc_new = _sample_scores(page_table, qi_s, wi_s, ki_s, cache_kidx[l], pp)
    idx_list = _sample_select(sc_past.reshape(Ns, n_pages * ps), sc_new.reshape(Ns, LANES), Tn)
    attn_s = _sample_attention(page_table, q_s, k_s, v_s, idx_list, rel_bias, cache_k[l], cache_v[l])
    y_s, h_s = _ssm(zs, c_u // LANES, d_ssm, 1, _pack_state(state_ssm_re[l], state_ssm_im[l])[:, None],
                    _ssm_tables(*ssm_w, Tn), Tn, False)
    hr_s, hi_s = _unpack_state(h_s[:, 0], G, STATE_DIM)
    out_s = tail(xs2, attn_s.reshape(Ns, d_attn), y_s, zs).reshape(Bd, Tn, D)

    sdt = state_ssm_re.dtype
    return (out_p, out_s,
            k_p.reshape(1, B, T // ps, ps, N_HEADS, HEAD_DIM), v_p.reshape(1, B, T // ps, ps, N_HEADS, HEAD_DIM),
            ki_p.reshape(1, B, T // ps, ps, IDX_DIM), hr_p.astype(sdt)[None], hi_p.astype(sdt)[None],
            k_s[None], v_s[None], ki_s[None], hr_s.astype(sdt)[None], hi_s.astype(sdt)[None])
```

```python
import functools
import math

import numpy as np
import jax
import jax.numpy as jnp
from jax import lax
from jax.experimental import pallas as pl
from jax.experimental.pallas import tpu as pltpu

f32, bf16, i32 = jnp.float32, jnp.bfloat16, jnp.int32

N_HEADS = 8
HEAD_DIM = 128
IDX_HEADS = 8
IDX_DIM = 64
TOPK_MAX = 256
NUM_BUCKETS = 32
MAX_DISTANCE = 128
SSM_GROUP = 16
STATE_DIM = 64
N_EXPERT_GROUPS = 4
EXPERTS_PER_GROUP = 8
TOP_K_EXPERT = 2
DEPTH = 1
ALPHA = (2 * DEPTH) ** 0.25
LN_EPS = 1e-5
INDEXER_SCALE = (IDX_HEADS ** -0.5) * (IDX_DIM ** -0.5)

LANES = 128
SUBLANES = 8
VMEM_LIMIT = 56 * 1024 * 1024
MOE_TILE = 256

INT_MIN = np.int32(-2 ** 31)
KEY_NEG_INF = np.int32(-2139095041)
NEG_BIG = -1e30


def _cparams(sem):
    return pltpu.CompilerParams(dimension_semantics=sem, vmem_limit_bytes=VMEM_LIMIT)


def _round_up(a, b):
    return (a + b - 1) // b * b


def _mm_nn_kernel(a_ref, b_ref, o_ref):
    o_ref[...] = jnp.dot(a_ref[...], b_ref[...], preferred_element_type=f32).astype(o_ref.dtype)


def _mm_nn(a, b, tm, tn, out_dtype=f32):
    M, K = a.shape
    N = b.shape[1]
    return pl.pallas_call(
        _mm_nn_kernel, grid=(M // tm, N // tn),
        in_specs=[pl.BlockSpec((tm, K), lambda i, j: (i, 0)), pl.BlockSpec((K, tn), lambda i, j: (0, j))],
        out_specs=pl.BlockSpec((tm, tn), lambda i, j: (i, j)),
        out_shape=jax.ShapeDtypeStruct((M, N), out_dtype),
        compiler_params=_cparams(("parallel", "arbitrary")), name="proj_nn")(a, b)


def _mm_nt_kernel(w_ref, x_ref, o_ref):
    o_ref[...] = lax.dot_general(w_ref[...], x_ref[...], (((1,), (1,)), ((), ())),
                                 preferred_element_type=f32).astype(o_ref.dtype)


def _mm_nt(w, x, tn, tm, out_dtype):
    n, K = w.shape
    M = x.shape[0]
    return pl.pallas_call(
        _mm_nt_kernel, grid=(M // tm, n // tn),
        in_specs=[pl.BlockSpec((tn, K), lambda i, j: (j, 0)), pl.BlockSpec((tm, K), lambda i, j: (i, 0))],
        out_specs=pl.BlockSpec((tn, tm), lambda i, j: (j, i)),
        out_shape=jax.ShapeDtypeStruct((n, M), out_dtype),
        compiler_params=_cparams(("parallel", "arbitrary")), name="proj_nt")(w, x)


def _order_key(x):
    b = lax.bitcast_convert_type(x + 0.0, i32)
    return b ^ ((b >> 31) & np.int32(0x7FFFFFFF))


def _bucket_starts():
    me = NUM_BUCKETS // 2
    d = np.arange(1, 4 * MAX_DISTANCE)
    large = me + (np.log(d.astype(np.float32) / np.float32(me)) / np.float32(math.log(MAX_DISTANCE / me))
                  * np.float32(NUM_BUCKETS - me)).astype(np.int32)
    bucket = np.where(d < me, d, np.minimum(large, NUM_BUCKETS - 1))
    return [int(d[np.argmax(bucket >= b)]) for b in range(me + 1, NUM_BUCKETS)]


def _rel_bucket(dist):
    me = NUM_BUCKETS // 2
    d = jnp.maximum(dist, 0)
    large = me
    for start in _bucket_starts():
        large = large + jnp.where(d >= start, 1, 0)
    return jnp.where(d < me, d, large)


def _select_rows(keys_sc, n_rows, k_top, idx_bits, slab):
    tq = keys_sc.shape[1]
    n_slabs = n_rows // slab

    def count(pred):
        def body(i, cnt):
            r0 = pl.multiple_of(i * slab, slab)
            blk = keys_sc[pl.ds(r0, slab), :]
            hit = jnp.where(pred(blk, r0), 1, 0).astype(i32)
            return cnt + jnp.sum(hit.reshape(slab // SUBLANES, SUBLANES, tq), axis=0)
        cnt = lax.fori_loop(0, n_slabs, body, jnp.zeros((SUBLANES, tq), i32))
        return jnp.sum(cnt, axis=0, keepdims=True)

    def count_ge(cand):
        cb = jnp.broadcast_to(cand, (slab, tq))
        return count(lambda blk, r0: blk >= cb)

    zero = jnp.zeros((1, tq), i32)
    thr = jnp.where(count_ge(zero) >= k_top, zero, jnp.full((1, tq), INT_MIN, i32))

    def bit_body(it, thr):
        cand = thr | lax.shift_left(np.int32(1), np.int32(30) - it)
        return jnp.where(count_ge(cand) >= k_top, cand, thr)

    thr = lax.fori_loop(0, 31, bit_body, thr)
    thr = jnp.maximum(thr, KEY_NEG_INF)
    n_ge = count_ge(thr)

    @pl.when(jnp.max(n_ge) > k_top)
    def _():
        need = k_top - count_ge(thr + 1)
        tb = jnp.broadcast_to(thr, (slab, tq))

        def count_eq_below(m):
            mb = jnp.broadcast_to(m, (slab, tq))
            return count(lambda blk, r0: (blk == tb) & (r0 + lax.broadcasted_iota(i32, (slab, tq), 0) < mb))

        def idx_body(it, m):
            cand = m | lax.shift_left(np.int32(1), np.int32(idx_bits - 1) - it)
            return jnp.where(count_eq_below(cand) < need, cand, m)

        m = lax.fori_loop(0, idx_bits, idx_body, zero)
        mb = jnp.broadcast_to(m, (slab, tq))

        def demote(i, _):
            r0 = pl.multiple_of(i * slab, slab)
            blk = keys_sc[pl.ds(r0, slab), :]
            drop = (blk == tb) & (r0 + lax.broadcasted_iota(i32, (slab, tq), 0) > mb)
            keys_sc[pl.ds(r0, slab), :] = jnp.where(drop, blk - 1, blk)
            return 0

        lax.fori_loop(0, n_slabs, demote, 0)

    return thr


def _pattn_kernel(qiT_ref, wiT_ref, ki_ref, qT_ref, k_ref, vT_ref, bias_ref, o_ref,
                  keys_sc, thr_sc, msk_sc, m_sc, l_sc, acc_sc, *, k_top, tq, idx_bits):
    qb, kb = _folded_blocks(pl.program_id(1), pl.program_id(2), pl.num_programs(2) - 1)

    @pl.when(kb == 0)
    def _():
        def chunk(c, _):
            r0 = pl.multiple_of(c * tq, tq)
            ki_c = ki_ref[0, pl.ds(r0, tq), :]
            score = jnp.zeros((tq, tq), f32)
            for h in range(IDX_HEADS):
                s = jnp.dot(ki_c, qiT_ref[h * IDX_DIM:(h + 1) * IDX_DIM, :], preferred_element_type=f32)
                score = score + wiT_ref[h:h + 1, :] * jnp.maximum(s, 0.0)
            kpos = r0 + lax.broadcasted_iota(i32, (tq, tq), 0)
            qpos = qb * tq + lax.broadcasted_iota(i32, (tq, tq), 1)
            keys_sc[pl.ds(r0, tq), :] = jnp.where(kpos <= qpos, _order_key(score), INT_MIN)
            return 0

        lax.fori_loop(0, qb + 1, chunk, 0)
        thr_sc[...] = _select_rows(keys_sc, (qb + 1) * tq, k_top, idx_bits, tq)
        m_sc[...] = jnp.full(m_sc.shape, NEG_BIG, f32)
        l_sc[...] = jnp.zeros(l_sc.shape, f32)
        acc_sc[...] = jnp.zeros(acc_sc.shape, f32)

    r0 = pl.multiple_of(kb * tq, tq)
    msk_sc[...] = jnp.where(keys_sc[pl.ds(r0, tq), :] >= thr_sc[...], 0.0, NEG_BIG)
    for h in range(N_HEADS):
        hs = slice(h * HEAD_DIM, (h + 1) * HEAD_DIM)
        for c in range(tq // LANES):
            cs = slice(c * LANES, (c + 1) * LANES)
            s = (jnp.dot(k_ref[:, hs], qT_ref[hs, cs], preferred_element_type=f32)
                 + bias_ref[h, 0, :, cs] + msk_sc[:, cs])
            m_old = m_sc[h:h + 1, cs]
            m_new = jnp.maximum(m_old, jnp.max(s, axis=0, keepdims=True))
            p = jnp.exp(s - m_new)
            alpha = jnp.exp(m_old - m_new)
            l_sc[h:h + 1, cs] = alpha * l_sc[h:h + 1, cs] + jnp.sum(p, axis=0, keepdims=True)
            acc_sc[hs, cs] = alpha * acc_sc[hs, cs] + jnp.dot(vT_ref[hs, :], p.astype(bf16),
                                                              preferred_element_type=f32)
            m_sc[h:h + 1, cs] = m_new

    @pl.when(kb == qb)
    def _():
        for h in range(N_HEADS):
            hs = slice(h * HEAD_DIM, (h + 1) * HEAD_DIM)
            o_ref[:, hs] = jnp.transpose(acc_sc[hs, :] / l_sc[h:h + 1, :])


def _folded_blocks(i, j, nq):
    first = j <= i
    return jnp.where(first, i, nq - 1 - i), jnp.where(first, j, j - i - 1)


def _prompt_attention(qT, vT, qiT, wiT, ki, k, bias_tiles, B, T, tq):
    nq = T // tq
    assert nq % 2 == 0
    k_top = min(TOPK_MAX, T // 4)
    d_attn = N_HEADS * HEAD_DIM
    kern = functools.partial(_pattn_kernel, k_top=k_top, tq=tq, idx_bits=max(1, (T - 1).bit_length()))
    qblk = lambda i, j: _folded_blocks(i, j, nq)[0]
    kblk = lambda i, j: _folded_blocks(i, j, nq)[1]
    col = lambda b, i, j: (0, b * nq + qblk(i, j))
    return pl.pallas_call(
        kern, grid=(B, nq // 2, nq + 1),
        in_specs=[
            pl.BlockSpec((IDX_HEADS * IDX_DIM, tq), col),
            pl.BlockSpec((IDX_HEADS, tq), col),
            pl.BlockSpec((1, T, IDX_DIM), lambda b, i, j: (b, 0, 0)),
            pl.BlockSpec((d_attn, tq), col),
            pl.BlockSpec((tq, d_attn), lambda b, i, j: (b * nq + kblk(i, j), 0)),
            pl.BlockSpec((d_attn, tq), lambda b, i, j: (0, b * nq + kblk(i, j))),
            pl.BlockSpec((N_HEADS, 1, tq, tq), lambda b, i, j: (0, jnp.minimum(qblk(i, j) - kblk(i, j), 2), 0, 0)),
        ],
        out_specs=pl.BlockSpec((tq, d_attn), lambda b, i, j: (b * nq + qblk(i, j), 0)),
        out_shape=jax.ShapeDtypeStruct((B * T, d_attn), f32),
        scratch_shapes=[pltpu.VMEM((T, tq), i32), pltpu.VMEM((1, tq), i32), pltpu.VMEM((tq, tq), f32),
                        pltpu.VMEM((N_HEADS, tq), f32), pltpu.VMEM((N_HEADS, tq), f32),
                        pltpu.VMEM((d_attn, tq), f32)],
        compiler_params=_cparams(("parallel", "arbitrary", "arbitrary")), name="prompt_attn",
    )(qiT, wiT, ki, qT, k, vT, bias_tiles)


def _bias_lookup_kernel(rb_ref, bkt_ref, o_ref):
    bkt = bkt_ref[0]
    for h in range(N_HEADS):
        acc = jnp.zeros(bkt.shape, f32)
        for b in range(NUM_BUCKETS):
            acc = jnp.where(bkt == b, rb_ref[b, h], acc)
        o_ref[h, 0] = acc


def _prompt_bias_tiles(rel_bias, tq):
    s = jnp.arange(tq, dtype=i32)[:, None]
    t = jnp.arange(tq, dtype=i32)[None, :]
    bkt = jnp.stack([_rel_bucket(d * tq + t - s) for d in range(3)]).astype(i32)
    return pl.pallas_call(
        _bias_lookup_kernel, grid=(3,),
        in_specs=[pl.BlockSpec(memory_space=pltpu.SMEM), pl.BlockSpec((1, tq, tq), lambda d: (d, 0, 0))],
        out_specs=pl.BlockSpec((N_HEADS, 1, tq, tq), lambda d: (0, d, 0, 0)),
        out_shape=jax.ShapeDtypeStruct((N_HEADS, 3, tq, tq), f32),
        compiler_params=_cparams(("parallel",)), name="bias_tiles")(rel_bias.astype(f32), bkt)


def _ssm_tables(lam_re, lam_im, log_step, b_re, b_im, c_re, c_im, d, L):
    hp = lax.Precision.HIGHEST
    G, P = lam_re.shape
    H = b_re.shape[-1]
    lr, li = lam_re.astype(f32), lam_im.astype(f32)
    step = jnp.exp(log_step.astype(f32))[:, None]
    taus = jnp.arange(L + 1, dtype=f32)[:, None, None]
    mag = jnp.exp(lr * step * taus)
    pr, pi = mag * jnp.cos(li * step * taus), mag * jnp.sin(li * step * taus)
    x, y, den = pr[1] - 1.0, pi[1], lr * lr + li * li
    fr, fi = (x * lr + y * li) / den, (y * lr - x * li) / den
    bbr = fr[..., None] * b_re - fi[..., None] * b_im
    bbi = fr[..., None] * b_im + fi[..., None] * b_re
    clr = c_re[None] * pr[:, :, None, :] - c_im[None] * pi[:, :, None, :]
    cli = c_re[None] * pi[:, :, None, :] + c_im[None] * pr[:, :, None, :]
    kern = (jnp.einsum('tghp,gpk->tghk', clr[:L], bbr, precision=hp)
            - jnp.einsum('tghp,gpk->tghk', cli[:L], bbi, precision=hp))
    kern = kern.at[0].add(jnp.eye(H, dtype=f32)[None] * d.astype(f32)[:, :, None])
    GB = LANES // H
    J = G // GB
    rev = jnp.arange(L - 1, -1, -1)
    wr = pr[rev][:, :, :, None] * bbr[None] - pi[rev][:, :, :, None] * bbi[None]
    wi = pr[rev][:, :, :, None] * bbi[None] + pi[rev][:, :, :, None] * bbr[None]
    ka = jnp.transpose(kern.reshape(L, J, GB, H, H), (1, 0, 2, 4, 3)).reshape(J, L, LANES, H)
    to_wa = lambda a: jnp.transpose(a.reshape(L, J, GB, P, H), (1, 0, 2, 4, 3)).reshape(J, L, LANES, P)
    wa = jnp.stack([to_wa(wr), to_wa(wi)], axis=1)
    to_va = lambda a: jnp.transpose(a.reshape(L, J, GB, H, P), (1, 0, 2, 4, 3)).reshape(J, L, GB * P, H)
    va = jnp.stack([to_va(clr[1:]), -to_va(cli[1:])], axis=1)
    mT, w, v = pl.pallas_call(
        functools.partial(_ssm_expand_kernel, L=L), grid=(J,),
        in_specs=[pl.BlockSpec((1, L, LANES, H), lambda j: (j, 0, 0, 0)),
                  pl.BlockSpec((1, 2, L, LANES, P), lambda j: (j, 0, 0, 0, 0)),
                  pl.BlockSpec((1, 2, L, GB * P, H), lambda j: (j, 0, 0, 0, 0))],
        out_specs=[pl.BlockSpec((1, L * LANES, L * LANES), lambda j: (j, 0, 0)),
                   pl.BlockSpec((1, L * LANES, 2 * GB * P), lambda j: (j, 0, 0)),
                   pl.BlockSpec((1, 2 * GB * P, L * LANES), lambda j: (j, 0, 0))],
        out_shape=[jax.ShapeDtypeStruct((J, L * LANES, L * LANES), bf16),
                   jax.ShapeDtypeStruct((J, L * LANES, 2 * GB * P), bf16),
                   jax.ShapeDtypeStruct((J, 2 * GB * P, L * LANES), bf16)],
        compiler_params=_cparams(("parallel",)), name="ssm_tables")(ka, wa, va)
    lam = jnp.concatenate([pr[L].reshape(J, 1, GB * P), pi[L].reshape(J, 1, GB * P)], axis=2)
    return mT, w, v, lam


def _ssm_expand_kernel(ka_ref, wa_ref, va_ref, mt_ref, w_ref, v_ref, *, L):
    H = ka_ref.shape[3]
    P = wa_ref.shape[4]
    S = v_ref.shape[1] // 2

    def repeat(n, width):
        return jnp.where(lax.broadcasted_iota(i32, (n, width), 1) % n == lax.broadcasted_iota(i32, (n, width), 0),
                         1.0, 0.0).astype(bf16)

    def same_group(rows, rdiv, cols, cdiv):
        return (lax.broadcasted_iota(i32, (rows, cols), 0) // rdiv) == (lax.broadcasted_iota(i32, (rows, cols), 1) // cdiv)

    rep_h, rep_p = repeat(H, LANES), repeat(P, S)
    m_mask, w_mask, v_mask = same_group(LANES, H, LANES, H), same_group(LANES, H, S, P), same_group(S, P, LANES, H)
    zero = jnp.zeros((LANES, LANES), bf16)
    for tau in range(L):
        tile = jnp.where(m_mask, jnp.dot(ka_ref[0, tau].astype(bf16), rep_h, preferred_element_type=f32), 0.0).astype(bf16)
        for s in range(L - tau):
            mt_ref[0, s * LANES:(s + 1) * LANES, (s + tau) * LANES:(s + tau + 1) * LANES] = tile
    for s in range(L):
        for t in range(s):
            mt_ref[0, s * LANES:(s + 1) * LANES, t * LANES:(t + 1) * LANES] = zero
    for half in range(2):
        for s in range(L):
            wt = jnp.dot(wa_ref[0, half, s].astype(bf16), rep_p, preferred_element_type=f32)
            w_ref[0, s * LANES:(s + 1) * LANES, half * S:(half + 1) * S] = jnp.where(w_mask, wt, 0.0).astype(bf16)
            vt = jnp.dot(va_ref[0, half, s].astype(bf16), rep_h, preferred_element_type=f32)
            v_ref[0, half * S:(half + 1) * S, s * LANES:(s + 1) * LANES] = jnp.where(v_mask, vt, 0.0).astype(bf16)


def _ssm_kernel(u_ref, mt_ref, w_ref, v_ref, lam_ref, h0_ref, y_ref, hout_ref, uc_sc, s_sc, hin_sc, yc_sc,
                *, L, rows, seq):
    S = s_sc.shape[1] // 2
    for s in range(L):
        uc_sc[:, s * LANES:(s + 1) * LANES] = u_ref[pl.ds(s, rows, stride=L), :].astype(bf16)
    uc = uc_sc[...]
    s_sc[...] = jnp.dot(uc, w_ref[0], preferred_element_type=f32)
    lr, li = lam_ref[0, :, 0:S], lam_ref[0, :, S:2 * S]
    if seq:
        def body(c, carry):
            hr, hi = carry
            hin_sc[pl.ds(c, 1), 0:S] = hr
            hin_sc[pl.ds(c, 1), S:2 * S] = hi
            sr, si = s_sc[pl.ds(c, 1), 0:S], s_sc[pl.ds(c, 1), S:2 * S]
            return lr * hr - li * hi + sr, lr * hi + li * hr + si

        hr, hi = lax.fori_loop(0, rows, body, (h0_ref[0, 0, :, 0:S], h0_ref[0, 0, :, S:2 * S]))
    else:
        hr0, hi0 = h0_ref[0, 0, :, 0:S], h0_ref[0, 0, :, S:2 * S]
        hin_sc[...] = h0_ref[0, 0]
        hr = lr * hr0 - li * hi0 + s_sc[:, 0:S]
        hi = lr * hi0 + li * hr0 + s_sc[:, S:2 * S]
    hout_ref[0, 0, :, 0:S] = hr
    hout_ref[0, 0, :, S:2 * S] = hi
    yc_sc[...] = (jnp.dot(uc, mt_ref[0], preferred_element_type=f32)
                  + jnp.dot(hin_sc[...].astype(bf16), v_ref[0], preferred_element_type=f32))
    for t in range(L):
        y_ref[pl.ds(t, rows, stride=L), :] = yc_sc[:, t * LANES:(t + 1) * LANES]


def _ssm(zn, u_blk0, d_ssm, n_steps, h0, tables, L, seq):
    mT, w, v, lam = tables
    N = zn.shape[0]
    J = d_ssm // LANES
    tok = N // n_steps
    rows = tok // L
    rows_h = h0.shape[2]
    S2 = h0.shape[3]
    kern = functools.partial(_ssm_kernel, L=L, rows=rows, seq=seq)
    return pl.pallas_call(
        kern, grid=(J, n_steps),
        in_specs=[
            pl.BlockSpec((tok, LANES), lambda j, b: (b, u_blk0 + j)),
            pl.BlockSpec((1, L * LANES, L * LANES), lambda j, b: (j, 0, 0)),
            pl.BlockSpec((1, L * LANES, S2), lambda j, b: (j, 0, 0)),
            pl.BlockSpec((1, S2, L * LANES), lambda j, b: (j, 0, 0)),
            pl.BlockSpec((1, 1, S2), lambda j, b: (j, 0, 0)),
            pl.BlockSpec((1, 1, rows_h, S2), lambda j, b: (j, b, 0, 0)),
        ],
        out_specs=[pl.BlockSpec((tok, LANES), lambda j, b: (b, j)),
                   pl.BlockSpec((1, 1, rows_h, S2), lambda j, b: (j, b, 0, 0))],
        out_shape=[jax.ShapeDtypeStruct((N, d_ssm), f32),
                   jax.ShapeDtypeStruct(h0.shape, f32)],
        scratch_shapes=[pltpu.VMEM((rows, L * LANES), bf16), pltpu.VMEM((rows, S2), f32),
                        pltpu.VMEM((rows, S2), f32), pltpu.VMEM((rows, L * LANES), f32)],
        compiler_params=_cparams(("parallel", "arbitrary")), name="ssm",
    )(zn, mT, w, v, lam, h0)


def _pack_state(a_re, a_im):
    Bn, G, P = a_re.shape
    J = G * SSM_GROUP // LANES
    blk = lambda a: jnp.transpose(a.astype(f32).reshape(Bn, J, (G // J) * P), (1, 0, 2))
    return jnp.concatenate([blk(a_re), blk(a_im)], axis=2)


def _unpack_state(h, G, P):
    J, Bn, S2 = h.shape
    un = lambda a: jnp.transpose(a, (1, 0, 2)).reshape(Bn, G, P)
    return un(h[:, :, :S2 // 2]), un(h[:, :, S2 // 2:])


def _sigmoid(x):
    return 1.0 / (1.0 + jnp.exp(-x))


def _gelu_tanh(x):
    return 0.5 * x * (1.0 + jnp.tanh(math.sqrt(2.0 / math.pi) * (x + 0.044715 * (x * x * x))))


def _layer_norm(r, g, b):
    mu = jnp.mean(r, axis=-1, keepdims=True)
    c = r - mu
    var = jnp.mean(c * c, axis=-1, keepdims=True)
    return c * lax.rsqrt(var + LN_EPS) * g + b


def _merge_kernel(x_ref, attn_ref, y_ref, ga_ref, gb_ref, wa_ref, wg_ref, ws_ref, wo_ref, g_ref, b_ref, o_ref, or_ref):
    a = jnp.dot(attn_ref[...].astype(bf16), wa_ref[...], preferred_element_type=f32)
    z = _gelu_tanh(y_ref[...])
    gl = jnp.dot(z.astype(bf16), wg_ref[...], preferred_element_type=f32)
    s = jnp.dot((z * _sigmoid(gl)).astype(bf16), ws_ref[...], preferred_element_type=f32)
    m = _sigmoid(ga_ref[...]) * a + _sigmoid(gb_ref[...]) * s
    mix = jnp.dot(m.astype(bf16), wo_ref[...], preferred_element_type=f32)
    o_ref[...] = _layer_norm(ALPHA * x_ref[...] + mix, g_ref[...], b_ref[...])
    _store_token_rows(or_ref, o_ref)


def _store_token_rows(rows_ref, src_ref):
    n, D = src_ref.shape
    for c in range(D // LANES):
        rows_ref[pl.ds(c, n, stride=D // LANES), :] = src_ref[:, c * LANES:(c + 1) * LANES]


def _load_token_rows(dst_ref, rows_ref, n):
    D = dst_ref.shape[1]
    for c in range(D // LANES):
        dst_ref[:, c * LANES:(c + 1) * LANES] = rows_ref[pl.ds(c, n, stride=D // LANES), :].astype(dst_ref.dtype)


def _merge(x, attn, ssm_y, zn, ga_blk, gb_blk, wa, wg, ws, wo, g, b, tm):
    N, D = x.shape
    Dh = attn.shape[1]
    const = lambda shape: pl.BlockSpec(shape, lambda i: (0, 0), pipeline_mode=pl.Buffered(1))
    return pl.pallas_call(
        _merge_kernel, grid=(N // tm,),
        in_specs=[pl.BlockSpec((tm, D), lambda i: (i, 0)),
                  pl.BlockSpec((tm, Dh), lambda i: (i, 0)),
                  pl.BlockSpec((tm, Dh), lambda i: (i, 0)),
                  pl.BlockSpec((tm, D), lambda i: (i, ga_blk)),
                  pl.BlockSpec((tm, D), lambda i: (i, gb_blk)),
                  const((Dh, D)), const((Dh, Dh)), const((Dh, D)), const((D, D)),
                  const((1, D)), const((1, D))],
        out_specs=[pl.BlockSpec((tm, D), lambda i: (i, 0)),
                   pl.BlockSpec((tm * (D // LANES), LANES), lambda i: (i, 0))],
        out_shape=[jax.ShapeDtypeStruct((N, D), f32), jax.ShapeDtypeStruct((N * (D // LANES), LANES), f32)],
        compiler_params=_cparams(("parallel",)), name="merge_ln1",
    )(x, attn, ssm_y, zn, zn, wa, wg, ws, wo, g, b)


ROUTE_E1, ROUTE_E2, ROUTE_W1, ROUTE_W2, ROUTE_R1, ROUTE_R2 = range(6)


def _router_kernel(x_ref, w_ref, b_ref, o_ref, route_ref, cnt_ref, run_sc):
    E = N_EXPERT_GROUPS * EXPERTS_PER_GROUP

    @pl.when(pl.program_id(0) == 0)
    def _():
        run_sc[...] = jnp.zeros(run_sc.shape, f32)

    logit = jnp.dot(x_ref[...], w_ref[...], preferred_element_type=f32, precision=lax.Precision.HIGHEST) + b_ref[...]
    lane = lax.broadcasted_iota(i32, logit.shape, 1)
    is_g = (lane >= E) & (lane < E + N_EXPERT_GROUPS)
    glog = jnp.where(is_g, logit, -jnp.inf)
    g_max = jnp.max(glog, axis=1, keepdims=True)
    g_idx = jnp.min(jnp.where(glog == g_max, lane, 4 * LANES), axis=1, keepdims=True) - E
    g_w = 1.0 / jnp.sum(jnp.where(is_g, jnp.exp(glog - g_max), 0.0), axis=1, keepdims=True)
    in_g = (lane < E) & (lane // EXPERTS_PER_GROUP == g_idx)
    e1 = jnp.where(in_g, logit, -jnp.inf)
    v1 = jnp.max(e1, axis=1, keepdims=True)
    i1 = jnp.min(jnp.where(e1 == v1, lane, 4 * LANES), axis=1, keepdims=True)
    e2 = jnp.where(lane == i1, -jnp.inf, e1)
    v2 = jnp.max(e2, axis=1, keepdims=True)
    i2 = jnp.min(jnp.where(e2 == v2, lane, 4 * LANES), axis=1, keepdims=True)
    t = jnp.exp(v2 - v1)
    w1 = g_w / (1.0 + t)
    w2 = g_w * t / (1.0 + t)
    o_ref[...] = jnp.where(lane == i1, w1, 0.0) + jnp.where(lane == i2, w2, 0.0)
    tm = logit.shape[0]
    hit = jnp.where((lane == i1) | (lane == i2), 1.0, 0.0)
    before = jnp.where(lax.broadcasted_iota(i32, (tm, tm), 0) > lax.broadcasted_iota(i32, (tm, tm), 1), 1.0, 0.0)
    rank = jnp.dot(before.astype(bf16), hit.astype(bf16), preferred_element_type=f32) + run_sc[...]
    r1 = jnp.sum(jnp.where(lane == i1, rank, 0.0), axis=1, keepdims=True)
    r2 = jnp.sum(jnp.where(lane == i2, rank, 0.0), axis=1, keepdims=True)
    rec = jnp.zeros(logit.shape, f32)
    for pos, val in ((ROUTE_E1, i1.astype(f32)), (ROUTE_E2, i2.astype(f32)), (ROUTE_W1, w1), (ROUTE_W2, w2),
                     (ROUTE_R1, r1), (ROUTE_R2, r2)):
        rec = jnp.where(lane == pos, val, rec)
    route_ref[...] = rec
    run_sc[...] += jnp.sum(hit, axis=0, keepdims=True)
    cnt_ref[...] = run_sc[...]


def _router(x1, w_r, b_r, tm):
    N, D = x1.shape
    tile = pl.BlockSpec((tm, LANES), lambda i: (i, 0))
    return pl.pallas_call(
        _router_kernel, grid=(N // tm,),
        in_specs=[pl.BlockSpec((tm, D), lambda i: (i, 0)),
                  pl.BlockSpec((D, LANES), lambda i: (0, 0)),
                  pl.BlockSpec((1, LANES), lambda i: (0, 0))],
        out_specs=[tile, tile, pl.BlockSpec((1, LANES), lambda i: (0, 0))],
        out_shape=[jax.ShapeDtypeStruct((N, LANES), f32), jax.ShapeDtypeStruct((N, LANES), f32),
                   jax.ShapeDtypeStruct((1, LANES), f32)],
        scratch_shapes=[pltpu.VMEM((1, LANES), f32)],
        compiler_params=_cparams(("arbitrary",)), name="router",
    )(x1, w_r, b_r)


def _moe_routed_kernel(te_ref, nu_ref, inv_ref, x_hbm, wg_ref, wu_ref, wd_ref, o_ref, xbuf, xb_sc, y_sc, sem, *, R, C):
    t = pl.program_id(0)
    n_used = nu_ref[0]
    slot = t % 2

    def row_copy(tt, sl, j):
        tok = inv_ref[tt * R + j]
        return pltpu.make_async_copy(x_hbm.at[pl.ds(tok * C, C), :], xbuf.at[sl, pl.ds(j * C, C), :], sem.at[sl])

    def issue(tt, sl):
        def body(j, _):
            row_copy(tt, sl, j).start()
            return 0
        lax.fori_loop(0, R, body, 0)

    @pl.when(t == 0)
    def _():
        issue(0, 0)

    @pl.when(t + 1 < n_used)
    def _():
        issue(t + 1, 1 - slot)

    @pl.when(t < n_used)
    def _():
        def wait_body(j, _):
            row_copy(t, slot, j).wait()
            return 0
        lax.fori_loop(0, R, wait_body, 0)
        _load_token_rows(xb_sc, xbuf.at[slot], R)
        xb = xb_sc[...]
        hg = jnp.dot(xb, wg_ref[0], preferred_element_type=f32)
        hu = jnp.dot(xb, wu_ref[0], preferred_element_type=f32)
        h = hg * _sigmoid(hg) * hu
        y_sc[...] = jnp.dot(h.astype(bf16), wd_ref[0], preferred_element_type=f32)
        _store_token_rows(o_ref, y_sc)

    @pl.when(t >= n_used)
    def _():
        o_ref[...] = jnp.zeros(o_ref.shape, f32)


def _moe_combine_kernel(p1_ref, p2_ref, x_ref, route_ref, g_ref, b_ref, y_hbm, o_ref, ybuf, r_sc, sem, *, tm, C):
    i = pl.program_id(0)
    n = pl.num_programs(0)
    slot = i % 2

    def row_copies(ii, sl, j):
        a = pltpu.make_async_copy(y_hbm.at[pl.ds(p1_ref[ii * tm + j] * C, C), :], ybuf.at[sl, 0, pl.ds(j * C, C), :], sem.at[sl])
        b = pltpu.make_async_copy(y_hbm.at[pl.ds(p2_ref[ii * tm + j] * C, C), :], ybuf.at[sl, 1, pl.ds(j * C, C), :], sem.at[sl])
        return a, b

    def issue(ii, sl):
        def body(j, _):
            a, b = row_copies(ii, sl, j)
            a.start()
            b.start()
            return 0
        lax.fori_loop(0, tm, body, 0)

    @pl.when(i == 0)
    def _():
        issue(0, 0)

    @pl.when(i + 1 < n)
    def _():
        issue(i + 1, 1 - slot)

    def wait_body(j, _):
        a, b = row_copies(i, slot, j)
        a.wait()
        b.wait()
        return 0

    lax.fori_loop(0, tm, wait_body, 0)
    route = route_ref[...]
    lane = lax.broadcasted_iota(i32, route.shape, 1)
    w1 = jnp.sum(jnp.where(lane == ROUTE_W1, route, 0.0), axis=1, keepdims=True)
    w2 = jnp.sum(jnp.where(lane == ROUTE_W2, route, 0.0), axis=1, keepdims=True)
    for c in range(C):
        cs = slice(c * LANES, (c + 1) * LANES)
        y1 = ybuf[slot, 0, pl.ds(c, tm, stride=C), :]
        y2 = ybuf[slot, 1, pl.ds(c, tm, stride=C), :]
        r_sc[:, cs] = ALPHA * x_ref[:, cs] + (w1 * y1 + w2 * y2)
    o_ref[...] = _layer_norm(r_sc[...], g_ref[...], b_ref[...])


def _moe_routed(x1, x1_rows, route, counts, wg, wu, wd, g, b, R, tm):
    N, D = x1.shape
    C = D // LANES
    E, _, De = wg.shape
    n_tiles = (2 * N + E * (R - 1)) // R + 1
    col = lambda k: route[:, k]
    e1, e2 = col(ROUTE_E1).astype(i32), col(ROUTE_E2).astype(i32)
    cnt = counts[0, :E].astype(i32)
    padded = (cnt + R - 1) // R * R
    seg_end = jnp.cumsum(padded)
    seg_start = seg_end - padded
    pos1 = seg_start[e1] + col(ROUTE_R1).astype(i32)
    pos2 = seg_start[e2] + col(ROUTE_R2).astype(i32)
    tile_start = jnp.arange(n_tiles, dtype=i32) * R
    tile_expert = jnp.minimum(jnp.sum((tile_start[:, None] >= seg_end[None, :]).astype(i32), axis=1), E - 1)
    n_used = (seg_end[E - 1] // R).reshape(1)
    tok = jnp.arange(N, dtype=i32)
    inv = jnp.zeros((n_tiles * R,), i32).at[pos1].set(tok).at[pos2].set(tok)
    wspec = lambda shape: pl.BlockSpec(shape, lambda t, te, nu, iv: (te[t], 0, 0))
    ys = pl.pallas_call(
        functools.partial(_moe_routed_kernel, R=R, C=C),
        grid_spec=pltpu.PrefetchScalarGridSpec(
            num_scalar_prefetch=3, grid=(n_tiles,),
            in_specs=[pl.BlockSpec(memory_space=pl.ANY), wspec((1, D, De)), wspec((1, D, De)), wspec((1, De, D))],
            out_specs=pl.BlockSpec((R * C, LANES), lambda t, te, nu, iv: (t, 0)),
            scratch_shapes=[pltpu.VMEM((2, R * C, LANES), f32), pltpu.VMEM((R, D), bf16), pltpu.VMEM((R, D), f32),
                            pltpu.SemaphoreType.DMA((2,))]),
        out_shape=jax.ShapeDtypeStruct((n_tiles * R * C, LANES), f32),
        compiler_params=pltpu.CompilerParams(dimension_semantics=("arbitrary",), vmem_limit_bytes=VMEM_LIMIT,
                                             disable_bounds_checks=True),
        name="moe_experts",
    )(tile_expert, n_used, inv, x1_rows, wg, wu, wd)
    return pl.pallas_call(
        functools.partial(_moe_combine_kernel, tm=tm, C=C),
        grid_spec=pltpu.PrefetchScalarGridSpec(
            num_scalar_prefetch=2, grid=(N // tm,),
            in_specs=[pl.BlockSpec((tm, D), lambda i, p1, p2: (i, 0)),
                      pl.BlockSpec((tm, LANES), lambda i, p1, p2: (i, 0)),
                      pl.BlockSpec((1, D), lambda i, p1, p2: (0, 0)),
                      pl.BlockSpec((1, D), lambda i, p1, p2: (0, 0)),
                      pl.BlockSpec(memory_space=pl.ANY)],
            out_specs=pl.BlockSpec((tm, D), lambda i, p1, p2: (i, 0)),
            scratch_shapes=[pltpu.VMEM((2, 2, tm * C, LANES), f32), pltpu.VMEM((tm, D), f32),
                            pltpu.SemaphoreType.DMA((2,))]),
        out_shape=jax.ShapeDtypeStruct((N, D), f32),
        compiler_params=pltpu.CompilerParams(dimension_semantics=("arbitrary",), vmem_limit_bytes=VMEM_LIMIT,
                                             disable_bounds_checks=True),
        name="moe_combine_ln2",
    )(pos1, pos2, x1, route, g, b, ys)


def _moe_kernel(x_ref, gate_ref, wg_ref, wu_ref, wd_ref, g_ref, b_ref, o_ref, xb_sc, acc_sc):
    e = pl.program_id(1)

    @pl.when(e == 0)
    def _():
        xb_sc[...] = x_ref[...].astype(bf16)
        acc_sc[...] = jnp.zeros(acc_sc.shape, f32)

    gates = gate_ref[...]
    lane = lax.broadcasted_iota(i32, gates.shape, 1)
    ge = jnp.sum(jnp.where(lane == e, gates, 0.0), axis=1, keepdims=True)
    xb = xb_sc[...]
    hg = jnp.dot(xb, wg_ref[0], preferred_element_type=f32)
    hu = jnp.dot(xb, wu_ref[0], preferred_element_type=f32)
    h = hg * _sigmoid(hg) * hu
    acc_sc[...] += jnp.dot((h * ge).astype(bf16), wd_ref[0], preferred_element_type=f32)

    @pl.when(e == pl.num_programs(1) - 1)
    def _():
        o_ref[...] = _layer_norm(ALPHA * x_ref[...] + acc_sc[...], g_ref[...], b_ref[...])


def _moe(x1, gates, wg, wu, wd, g, b, tm):
    N, D = x1.shape
    E, _, De = wg.shape
    return pl.pallas_call(
        _moe_kernel, grid=(N // tm, E),
        in_specs=[pl.BlockSpec((tm, D), lambda i, e: (i, 0)),
                  pl.BlockSpec((tm, LANES), lambda i, e: (i, 0)),
                  pl.BlockSpec((1, D, De), lambda i, e: (e, 0, 0)),
                  pl.BlockSpec((1, D, De), lambda i, e: (e, 0, 0)),
                  pl.BlockSpec((1, De, D), lambda i, e: (e, 0, 0)),
                  pl.BlockSpec((1, D), lambda i, e: (0, 0)),
                  pl.BlockSpec((1, D), lambda i, e: (0, 0))],
        out_specs=pl.BlockSpec((tm, D), lambda i, e: (i, 0)),
        out_shape=jax.ShapeDtypeStruct((N, D), f32),
        scratch_shapes=[pltpu.VMEM((tm, D), bf16), pltpu.VMEM((tm, D), f32)],
        compiler_params=_cparams(("parallel", "arbitrary")), name="moe_ln2",
    )(x1, gates, wg, wu, wd, g, b)


def _sscore_kernel(pt_ref, qi_ref, wi_ref, kn_ref, *rest, pp):
    pages, (o_ref, on_ref) = rest[:pp], rest[pp:]
    qi = qi_ref[0]
    wi = wi_ref[0]
    tn = qi.shape[0] // IDX_HEADS

    def scores(keys):
        s = lax.dot_general(qi, keys, (((1,), (1,)), ((), ())), preferred_element_type=f32)
        r = jnp.maximum(s, 0.0) * wi
        return jnp.sum(r.reshape(tn, IDX_HEADS, keys.shape[0]), axis=1)

    for j in range(pp):
        o_ref[0, :, j * LANES:(j + 1) * LANES] = scores(pages[j][0].astype(bf16))

    @pl.when(pl.program_id(1) == 0)
    def _():
        on_ref[0] = scores(kn_ref[0])


def _sample_scores(page_table, qi, wi, ki_new, cache_kidx, pp):
    Bd, Tn = qi.shape[:2]
    n_pages = page_table.shape[1]
    ps = cache_kidx.shape[1]
    rows = Tn * IDX_HEADS
    qi2 = qi.reshape(Bd, rows, IDX_DIM).astype(bf16)
    wi2 = wi.reshape(Bd, rows, 1)
    kn = jnp.pad(ki_new, ((0, 0), (0, LANES - Tn), (0, 0))).astype(bf16)
    page_spec = lambda j: pl.BlockSpec((1, ps, IDX_DIM), lambda b, p, pt: (pt[b, p * pp + j], 0, 0))
    grid_spec = pltpu.PrefetchScalarGridSpec(
        num_scalar_prefetch=1, grid=(Bd, n_pages // pp),
        in_specs=[pl.BlockSpec((1, rows, IDX_DIM), lambda b, p, pt: (b, 0, 0)),
                  pl.BlockSpec((1, rows, 1), lambda b, p, pt: (b, 0, 0)),
                  pl.BlockSpec((1, LANES, IDX_DIM), lambda b, p, pt: (b, 0, 0))]
                 + [page_spec(j) for j in range(pp)],
        out_specs=[pl.BlockSpec((1, Tn, pp * ps), lambda b, p, pt: (b, 0, p)),
                   pl.BlockSpec((1, Tn, LANES), lambda b, p, pt: (b, 0, 0))])
    return pl.pallas_call(
        functools.partial(_sscore_kernel, pp=pp), grid_spec=grid_spec,
        out_shape=[jax.ShapeDtypeStruct((Bd, Tn, n_pages * ps), f32),
                   jax.ShapeDtypeStruct((Bd, Tn, LANES), f32)],
        compiler_params=_cparams(("parallel", "arbitrary")), name="sample_scores",
    )(page_table, qi2, wi2, kn, *([cache_kidx] * pp))


def _sselect_kernel(sp_ref, sn_ref, o_ref, keys_sc, *, k_top, tn, idx_bits):
    rows = sp_ref.shape[0]
    past = sp_ref.shape[1]
    n_tiles = past // LANES

    def load(j, _):
        c0 = pl.multiple_of(j * LANES, LANES)
        keys_sc[pl.ds(c0, LANES), :] = _order_key(jnp.transpose(sp_ref[:, pl.ds(c0, LANES)]))
        return 0

    lax.fori_loop(0, n_tiles, load, 0)
    new = _order_key(jnp.transpose(sn_ref[...]))
    jj = lax.broadcasted_iota(i32, new.shape, 0)
    qq = lax.broadcasted_iota(i32, new.shape, 1) % tn
    keys_sc[pl.ds(past, LANES), :] = jnp.where(jj <= qq, new, INT_MIN)
    thr = _select_rows(keys_sc, past + LANES, k_top, idx_bits, LANES)

    none = np.int32(2 ** 30)
    o_ref[...] = jnp.zeros(o_ref.shape, i32)
    slot = lax.broadcasted_iota(i32, o_ref.shape, 0)

    def slab(j, filled):
        c0 = pl.multiple_of(j * LANES, LANES)
        sel = keys_sc[pl.ds(c0, LANES), :] >= thr
        cand = jnp.where(sel, c0 + lax.broadcasted_iota(i32, sel.shape, 0), none)
        n_sel = jnp.sum(jnp.where(sel, 1, 0).astype(i32), axis=0, keepdims=True)

        def extract(i, cand):
            first = jnp.min(cand, axis=0, keepdims=True)
            o_ref[...] = jnp.where((slot == filled + i) & (first < none), first, o_ref[...])
            return jnp.where(cand == first, none, cand)

        lax.fori_loop(0, jnp.max(n_sel), extract, cand)
        return filled + n_sel

    lax.fori_loop(0, n_tiles + 1, slab, jnp.zeros((1, rows), i32))


def _sample_select(scores_past, scores_new, tn):
    assert scores_past.shape[0] <= LANES
    pad_rows = lambda a: jnp.pad(a, ((0, LANES - a.shape[0]), (0, 0)))
    scores_past, scores_new = pad_rows(scores_past), pad_rows(scores_new)
    R, past = scores_past.shape
    k_top = min(TOPK_MAX, (past + tn) // 4)
    assert past + 1 >= k_top
    kern = functools.partial(_sselect_kernel, k_top=k_top, tn=tn, idx_bits=(past + LANES - 1).bit_length())
    return pl.pallas_call(
        kern, grid=(1,),
        in_specs=[pl.BlockSpec((R, past), lambda i: (0, 0)), pl.BlockSpec((R, LANES), lambda i: (0, 0))],
        out_specs=pl.BlockSpec((k_top, R), lambda i: (0, 0)),
        out_shape=jax.ShapeDtypeStruct((k_top, R), i32),
        scratch_shapes=[pltpu.VMEM((past + LANES, R), i32)],
        compiler_params=_cparams(("arbitrary",)), name="sample_select",
    )(scores_past, scores_new)


def _sgather_kernel(idx_ref, pt_ref, qT_ref, idxv_ref, knew_ref, vnew_ref, rb_ref, ck_hbm, cv_hbm, o_ref,
                    kbuf, vbuf, ksem, vsem, *, K, tn, past, n_pages, ps):
    r = pl.program_id(0)
    n = pl.num_programs(0)
    slot = r % 2

    def row_copies(rr, sl, j):
        idx = jnp.minimum(idx_ref[rr * K + j], past - 1)
        phys = pt_ref[(rr // tn) * n_pages + idx // ps]
        return (pltpu.make_async_copy(ck_hbm.at[phys, idx % ps], kbuf.at[sl, j], ksem.at[sl]),
                pltpu.make_async_copy(cv_hbm.at[phys, idx % ps], vbuf.at[sl, j], vsem.at[sl]))

    def issue(rr, sl):
        def body(j, _):
            ck, cv = row_copies(rr, sl, j)
            ck.start()
            cv.start()
            return 0
        lax.fori_loop(0, K, body, 0)

    @pl.when(r == 0)
    def _():
        issue(0, 0)

    @pl.when(r + 1 < n)
    def _():
        issue(r + 1, 1 - slot)

    def wait_body(j, _):
        ck, cv = row_copies(r, slot, j)
        ck.wait()
        cv.wait()
        return 0

    lax.fori_loop(0, K, wait_body, 0)

    for t in range(tn):
        j = K - 1 - t
        pos = idx_ref[r * K + j]

        @pl.when(pos >= past)
        def _():
            kbuf[slot, j] = knew_ref[0, pos - past]
            vbuf[slot, j] = vnew_ref[0, pos - past]

    idxv = idxv_ref[0]
    k = kbuf[slot]
    v = vbuf[slot]
    pairs = jnp.dot(k.reshape(K * N_HEADS, HEAD_DIM).astype(bf16), qT_ref[0], preferred_element_type=f32)
    pairs = pairs.reshape(K, N_HEADS, LANES)
    diag = lax.broadcasted_iota(i32, pairs.shape, 1) == lax.broadcasted_iota(i32, pairs.shape, 2)
    logits = jnp.sum(jnp.where(diag, pairs, 0.0), axis=1)
    dist = past + r % tn - idxv
    onehot = jnp.where(_rel_bucket(dist) == lax.broadcasted_iota(i32, (K, LANES), 1), 1.0, 0.0)
    logits = logits + jnp.dot(onehot, rb_ref[...], preferred_element_type=f32, precision=lax.Precision.HIGHEST)
    p = jnp.exp(logits - jnp.max(logits, axis=0, keepdims=True))
    p = p / jnp.sum(p, axis=0, keepdims=True)
    pd = jnp.where(diag, jnp.broadcast_to(p[:, None, :], pairs.shape), 0.0).reshape(K * N_HEADS, LANES)
    spread = jnp.dot(pd.astype(bf16), jnp.ones((LANES, HEAD_DIM), bf16), preferred_element_type=f32)
    o_ref[0] = jnp.sum(spread.reshape(K, N_HEADS, HEAD_DIM) * v, axis=0)


def _sample_attention(page_table, q, k_new, v_new, idx_list, rel_bias, cache_k, cache_v):
    Bd, Tn = q.shape[:2]
    n_pages = page_table.shape[1]
    ps = cache_k.shape[1]
    past = n_pages * ps
    R = Bd * Tn
    K = idx_list.shape[0]
    idx = jnp.transpose(idx_list)[:R]
    qT = jnp.transpose((q * (HEAD_DIM ** -0.5)).reshape(R, N_HEADS, HEAD_DIM), (0, 2, 1))
    qT = jnp.pad(qT, ((0, 0), (0, 0), (0, LANES - N_HEADS))).astype(bf16)
    rb = jnp.pad(rel_bias.astype(f32), ((0, LANES - NUM_BUCKETS), (0, LANES - N_HEADS)))
    row = lambda r, idx_s, pt_s: (r, 0, 0)
    batch = lambda r, idx_s, pt_s: (r // Tn, 0, 0, 0)
    grid_spec = pltpu.PrefetchScalarGridSpec(
        num_scalar_prefetch=2, grid=(R,),
        in_specs=[pl.BlockSpec((1, HEAD_DIM, LANES), row),
                  pl.BlockSpec((1, K, 1), row),
                  pl.BlockSpec((1, Tn, N_HEADS, HEAD_DIM), batch),
                  pl.BlockSpec((1, Tn, N_HEADS, HEAD_DIM), batch),
                  pl.BlockSpec((LANES, LANES), lambda r, idx_s, pt_s: (0, 0)),
                  pl.BlockSpec(memory_space=pl.ANY),
                  pl.BlockSpec(memory_space=pl.ANY)],
        out_specs=pl.BlockSpec((1, N_HEADS, HEAD_DIM), row),
        scratch_shapes=[pltpu.VMEM((2, K, N_HEADS, HEAD_DIM), f32), pltpu.VMEM((2, K, N_HEADS, HEAD_DIM), f32),
                        pltpu.SemaphoreType.DMA((2,)), pltpu.SemaphoreType.DMA((2,))])
    kern = functools.partial(_sgather_kernel, K=K, tn=Tn, past=past, n_pages=n_pages, ps=ps)
    out = pl.pallas_call(
        kern, grid_spec=grid_spec,
        out_shape=jax.ShapeDtypeStruct((R, N_HEADS, HEAD_DIM), f32),
        compiler_params=pltpu.CompilerParams(dimension_semantics=("arbitrary",), vmem_limit_bytes=VMEM_LIMIT,
                                             disable_bounds_checks=True),
        name="sample_attn",
    )(idx.reshape(R * K), page_table.reshape(Bd * n_pages), qT, idx.reshape(R, K, 1), k_new, v_new, rb,
      cache_k, cache_v)
    return out.reshape(Bd, Tn, N_HEADS * HEAD_DIM)


def _pick(n, cands):
    for c in cands:
        if n % c == 0:
            return c
    return n


def kernel(x_prompt, x_sample, cache_k, cache_v, cache_kidx, state_ssm_re, state_ssm_im, page_table, rel_bias, w_in, ssm_lambda_re, ssm_lambda_im, ssm_log_step, ssm_b_re, ssm_b_im, ssm_c_re, ssm_c_im, ssm_d, w_glu, w_attn_out, w_ssm_out, w_o, ln1_g, ln1_b, w_group_router, b_group_router, w_expert_router, b_expert_router, w_exp_gate, w_exp_up, w_exp_down, ln2_g, ln2_b):
    B, T, D = x_prompt.shape
    Bd, Tn = x_sample.shape[:2]
    n_phys, ps = cache_k.shape[1:3]
    d_attn = N_HEADS * HEAD_DIM
    d_qi = IDX_HEADS * IDX_DIM
    d_ssm = D // 2
    G = d_ssm // SSM_GROUP
    E = N_EXPERT_GROUPS * EXPERTS_PER_GROUP
    l = 0
    w = w_in[l]
    o_q, o_k, o_v, o_qi = 0, d_attn, 2 * d_attn, 3 * d_attn
    o_ki = o_qi + d_qi
    o_wi = o_ki + IDX_DIM
    o_u = o_wi + IDX_HEADS
    o_ga = o_u + d_ssm
    o_gb = o_ga + D
    cols = lambda a, n: w[:, a:a + n]

    small = jnp.concatenate([cols(o_ki, IDX_DIM), cols(o_wi, IDX_HEADS) * INDEXER_SCALE,
                             jnp.zeros((D, LANES - IDX_DIM - IDX_HEADS), f32)], axis=1)
    w_n = jnp.concatenate([cols(o_ga, D), cols(o_gb, D), cols(o_k, d_attn), cols(o_v, d_attn), small, cols(o_u, d_ssm)],
                          axis=1)
    c_ga, c_gb, c_k = 0, D, 2 * D
    c_v = c_k + d_attn
    c_s = c_v + d_attn
    c_u = c_s + LANES
    n_cols = c_u + d_ssm
    tn_n = 768
    w_n = jnp.pad(w_n, ((0, 0), (0, _round_up(n_cols, tn_n) - n_cols))).astype(bf16)
    w_t = jnp.concatenate([cols(o_q, d_attn) * (HEAD_DIM ** -0.5), cols(o_v, d_attn), cols(o_qi, d_qi)], axis=1).T.astype(bf16)
    w_wi = jnp.pad((cols(o_wi, IDX_HEADS) * INDEXER_SCALE).T, ((0, 2 * SUBLANES - IDX_HEADS), (0, 0))).astype(bf16)

    wa, wg, ws, wo = (a[l].astype(bf16) for a in (w_attn_out, w_glu, w_ssm_out, w_o))
    w_r = jnp.concatenate([w_expert_router[l], w_group_router[l], jnp.zeros((D, LANES - E - N_EXPERT_GROUPS), f32)], axis=1)
    b_r = jnp.concatenate([b_expert_router[l], b_group_router[l], jnp.zeros((LANES - E - N_EXPERT_GROUPS,), f32)])[None, :]
    weg, weu, wed = (a[l].astype(bf16) for a in (w_exp_gate, w_exp_up, w_exp_down))
    g1, b1, g2, b2 = (a[l][None, :].astype(f32) for a in (ln1_g, ln1_b, ln2_g, ln2_b))
    ssm_w = (ssm_lambda_re[l], ssm_lambda_im[l], ssm_log_step[l], ssm_b_re[l], ssm_b_im[l], ssm_c_re[l], ssm_c_im[l], ssm_d[l])

    def tail(x2, attn, ssm_y, zn):
        n = x2.shape[0]
        x1, x1_rows = _merge(x2, attn, ssm_y, zn, c_ga // D, c_gb // D, wa, wg, ws, wo, g1, b1, _pick(n, (256, 128)))
        gates, route, counts = _router(x1, w_r, b_r, _pick(n, (512, 128)))
        if 2 * n >= E * MOE_TILE:
            return _moe_routed(x1, x1_rows, route, counts, weg, weu, wed, g2, b2, MOE_TILE, _pick(n, (256, 128)))
        return _moe(x1, gates, weg, weu, wed, g2, b2, _pick(n, (512, 128)))

    N = B * T
    xp2 = x_prompt.reshape(N, D)
    xpb = xp2.astype(bf16)
    tm = _pick(N, (1024, 512, 256, 128))
    zn = _mm_nn(xpb, w_n, tm, tn_n)
    zt = _mm_nt(w_t, xpb, 512, tm, bf16)
    wit = _mm_nt(w_wi, xpb, 2 * SUBLANES, tm, f32)
    k_p, v_p, ki_p = zn[:, c_k:c_k + d_attn], zn[:, c_v:c_v + d_attn], zn[:, c_s:c_s + IDX_DIM]
    tq = _pick(T, (256, 128))
    attn_p = _prompt_attention(zt[:d_attn], zt[d_attn:2 * d_attn], zt[2 * d_attn:], wit,
                               ki_p.reshape(B, T, IDX_DIM).astype(bf16), k_p.astype(bf16),
                               _prompt_bias_tiles(rel_bias, tq), B, T, tq)
    L = _pick(T, (16, 8, 4, 2))
    zeros_state = jnp.zeros((B, G, STATE_DIM), f32)
    y_p, h_p = _ssm(zn, c_u // LANES, d_ssm, B, _pack_state(zeros_state, zeros_state)[:, :, None, :],
                    _ssm_tables(*ssm_w, L), L, True)
    hr_p, hi_p = _unpack_state(h_p[:, :, 0, :], G, STATE_DIM)
    out_p = tail(xp2, attn_p, y_p, zn).reshape(B, T, D)

    Ns = Bd * Tn
    xs2 = x_sample.reshape(Ns, D)
    xsb = xs2.astype(bf16)
    zs = _mm_nn(xsb, w_n, Ns, tn_n)
    w_s = jnp.concatenate([cols(o_q, d_attn), cols(o_qi, d_qi)], axis=1)
    w_s = jnp.pad(w_s, ((0, 0), (0, _round_up(d_attn + d_qi, tn_n) - d_attn - d_qi))).astype(bf16)
    zq = _mm_nn(xsb, w_s, Ns, tn_n)
    q_s = zq[:, :d_attn].reshape(Bd, Tn, N_HEADS, HEAD_DIM)
    qi_s = zq[:, d_attn:d_attn + d_qi].reshape(Bd, Tn, IDX_HEADS, IDX_DIM)
    k_s = zs[:, c_k:c_k + d_attn].reshape(Bd, Tn, N_HEADS, HEAD_DIM)
    v_s = zs[:, c_v:c_v + d_attn].reshape(Bd, Tn, N_HEADS, HEAD_DIM)
    ki_s = zs[:, c_s:c_s + IDX_DIM].reshape(Bd, Tn, IDX_DIM)
    wi_s = zs[:, c_s + IDX_DIM:c_s + IDX_DIM + IDX_HEADS].reshape(Bd, Tn, IDX_HEADS)
    n_pages = page_table.shape[1]
    pp = _pick(n_pages, (16, 8, 4, 2, 1))
    sc_past, sc_new = _sample_scores(page_table, qi_s, wi_s, ki_s, cache_kidx[l], pp)
    idx_list = _sample_select(sc_past.reshape(Ns, n_pages * ps), sc_new.reshape(Ns, LANES), Tn)
    attn_s = _sample_attention(page_table, q_s, k_s, v_s, idx_list, rel_bias, cache_k[l], cache_v[l])
    y_s, h_s = _ssm(zs, c_u // LANES, d_ssm, 1, _pack_state(state_ssm_re[l], state_ssm_im[l])[:, None],
                    _ssm_tables(*ssm_w, Tn), Tn, False)
    hr_s, hi_s = _unpack_state(h_s[:, 0], G, STATE_DIM)
    out_s = tail(xs2, attn_s.reshape(Ns, d_attn), y_s, zs).reshape(Bd, Tn, D)

    sdt = state_ssm_re.dtype
    return (out_p, out_s,
            k_p.reshape(1, B, T // ps, ps, N_HEADS, HEAD_DIM), v_p.reshape(1, B, T // ps, ps, N_HEADS, HEAD_DIM),
            ki_p.reshape(1, B, T // ps, ps, IDX_DIM), hr_p.astype(sdt)[None], hi_p.astype(sdt)[None],
            k_s[None], v_s[None], ki_s[None], hr_s.astype(sdt)[None], hi_s.astype(sdt)[None])
```

```python
import functools
import math

import numpy as np
import jax
import jax.numpy as jnp
from jax import lax
from jax.experimental import pallas as pl
from jax.experimental.pallas import tpu as pltpu

f32, bf16, i32 = jnp.float32, jnp.bfloat16, jnp.int32

N_HEADS = 8
HEAD_DIM = 128
IDX_HEADS = 8
IDX_DIM = 64
TOPK_MAX = 256
NUM_BUCKETS = 32
MAX_DISTANCE = 128
SSM_GROUP = 16
STATE_DIM = 64
N_EXPERT_GROUPS = 4
EXPERTS_PER_GROUP = 8
TOP_K_EXPERT = 2
DEPTH = 1
ALPHA = (2 * DEPTH) ** 0.25
LN_EPS = 1e-5
INDEXER_SCALE = (IDX_HEADS ** -0.5) * (IDX_DIM ** -0.5)

LANES = 128
SUBLANES = 8
VMEM_LIMIT = 56 * 1024 * 1024
MOE_TILE = 256
DMA_UNROLL = 8

INT_MIN = np.int32(-2 ** 31)
KEY_NEG_INF = np.int32(-2139095041)
NEG_BIG = -1e30


def _cparams(sem):
    return pltpu.CompilerParams(dimension_semantics=sem, vmem_limit_bytes=VMEM_LIMIT)


def _round_up(a, b):
    return (a + b - 1) // b * b


def _mm_nn_kernel(a_ref, b_ref, o_ref):
    o_ref[...] = jnp.dot(a_ref[...], b_ref[...], preferred_element_type=f32).astype(o_ref.dtype)


def _mm_nn(a, b, tm, tn, out_dtype=f32):
    M, K = a.shape
    N = b.shape[1]
    return pl.pallas_call(
        _mm_nn_kernel, grid=(M // tm, N // tn),
        in_specs=[pl.BlockSpec((tm, K), lambda i, j: (i, 0)), pl.BlockSpec((K, tn), lambda i, j: (0, j))],
        out_specs=pl.BlockSpec((tm, tn), lambda i, j: (i, j)),
        out_shape=jax.ShapeDtypeStruct((M, N), out_dtype),
        compiler_params=_cparams(("parallel", "arbitrary")), name="proj_nn")(a, b)


def _mm_nt_kernel(w_ref, x_ref, o_ref):
    o_ref[...] = lax.dot_general(w_ref[...], x_ref[...], (((1,), (1,)), ((), ())),
                                 preferred_element_type=f32).astype(o_ref.dtype)


def _mm_nt(w, x, tn, tm, out_dtype):
    n, K = w.shape
    M = x.shape[0]
    return pl.pallas_call(
        _mm_nt_kernel, grid=(M // tm, n // tn),
        in_specs=[pl.BlockSpec((tn, K), lambda i, j: (j, 0)), pl.BlockSpec((tm, K), lambda i, j: (i, 0))],
        out_specs=pl.BlockSpec((tn, tm), lambda i, j: (j, i)),
        out_shape=jax.ShapeDtypeStruct((n, M), out_dtype),
        compiler_params=_cparams(("parallel", "arbitrary")), name="proj_nt")(w, x)


def _order_key(x):
    b = lax.bitcast_convert_type(x + 0.0, i32)
    return b ^ ((b >> 31) & np.int32(0x7FFFFFFF))


def _bucket_starts():
    me = NUM_BUCKETS // 2
    d = np.arange(1, 4 * MAX_DISTANCE)
    large = me + (np.log(d.astype(np.float32) / np.float32(me)) / np.float32(math.log(MAX_DISTANCE / me))
                  * np.float32(NUM_BUCKETS - me)).astype(np.int32)
    bucket = np.where(d < me, d, np.minimum(large, NUM_BUCKETS - 1))
    return [int(d[np.argmax(bucket >= b)]) for b in range(me + 1, NUM_BUCKETS)]


def _rel_bucket(dist):
    me = NUM_BUCKETS // 2
    d = jnp.maximum(dist, 0)
    large = me
    for start in _bucket_starts():
        large = large + jnp.where(d >= start, 1, 0)
    return jnp.where(d < me, d, large)


def _select_rows(keys_sc, n_rows, k_top, idx_bits, slab):
    tq = keys_sc.shape[1]
    n_slabs = n_rows // slab

    def count(pred):
        def body(i, cnt):
            r0 = pl.multiple_of(i * slab, slab)
            blk = keys_sc[pl.ds(r0, slab), :]
            hit = jnp.where(pred(blk, r0), 1, 0).astype(i32)
            return cnt + jnp.sum(hit.reshape(slab // SUBLANES, SUBLANES, tq), axis=0)
        cnt = lax.fori_loop(0, n_slabs, body, jnp.zeros((SUBLANES, tq), i32))
        return jnp.sum(cnt, axis=0, keepdims=True)

    def count_ge(cand):
        cb = jnp.broadcast_to(cand, (slab, tq))
        return count(lambda blk, r0: blk >= cb)

    zero = jnp.zeros((1, tq), i32)
    thr = jnp.where(count_ge(zero) >= k_top, zero, jnp.full((1, tq), INT_MIN, i32))

    def bit_body(it, thr):
        cand = thr | lax.shift_left(np.int32(1), np.int32(30) - it)
        return jnp.where(count_ge(cand) >= k_top, cand, thr)

    thr = lax.fori_loop(0, 31, bit_body, thr)
    thr = jnp.maximum(thr, KEY_NEG_INF)
    n_ge = count_ge(thr)

    @pl.when(jnp.max(n_ge) > k_top)
    def _():
        need = k_top - count_ge(thr + 1)
        tb = jnp.broadcast_to(thr, (slab, tq))

        def count_eq_below(m):
            mb = jnp.broadcast_to(m, (slab, tq))
            return count(lambda blk, r0: (blk == tb) & (r0 + lax.broadcasted_iota(i32, (slab, tq), 0) < mb))

        def idx_body(it, m):
            cand = m | lax.shift_left(np.int32(1), np.int32(idx_bits - 1) - it)
            return jnp.where(count_eq_below(cand) < need, cand, m)

        m = lax.fori_loop(0, idx_bits, idx_body, zero)
        mb = jnp.broadcast_to(m, (slab, tq))

        def demote(i, _):
            r0 = pl.multiple_of(i * slab, slab)
            blk = keys_sc[pl.ds(r0, slab), :]
            drop = (blk == tb) & (r0 + lax.broadcasted_iota(i32, (slab, tq), 0) > mb)
            keys_sc[pl.ds(r0, slab), :] = jnp.where(drop, blk - 1, blk)
            return 0

        lax.fori_loop(0, n_slabs, demote, 0)

    return thr


def _pattn_kernel(qiT_ref, wiT_ref, ki_ref, qT_ref, k_ref, vT_ref, bias_ref, o_ref,
                  keys_sc, thr_sc, msk_sc, m_sc, l_sc, acc_sc, *, k_top, tq, idx_bits):
    qb, kb = _folded_blocks(pl.program_id(1), pl.program_id(2), pl.num_programs(2) - 1)

    @pl.when(kb == 0)
    def _():
        def chunk(c, _):
            r0 = pl.multiple_of(c * tq, tq)
            ki_c = ki_ref[0, pl.ds(r0, tq), :]
            score = jnp.zeros((tq, tq), f32)
            for h in range(IDX_HEADS):
                s = jnp.dot(ki_c, qiT_ref[h * IDX_DIM:(h + 1) * IDX_DIM, :], preferred_element_type=f32)
                score = score + wiT_ref[h:h + 1, :] * jnp.maximum(s, 0.0)
            kpos = r0 + lax.broadcasted_iota(i32, (tq, tq), 0)
            qpos = qb * tq + lax.broadcasted_iota(i32, (tq, tq), 1)
            keys_sc[pl.ds(r0, tq), :] = jnp.where(kpos <= qpos, _order_key(score), INT_MIN)
            return 0

        lax.fori_loop(0, qb + 1, chunk, 0)
        thr_sc[...] = _select_rows(keys_sc, (qb + 1) * tq, k_top, idx_bits, tq)
        m_sc[...] = jnp.full(m_sc.shape, NEG_BIG, f32)
        l_sc[...] = jnp.zeros(l_sc.shape, f32)
        acc_sc[...] = jnp.zeros(acc_sc.shape, f32)

    r0 = pl.multiple_of(kb * tq, tq)
    msk_sc[...] = jnp.where(keys_sc[pl.ds(r0, tq), :] >= thr_sc[...], 0.0, NEG_BIG)
    for h in range(N_HEADS):
        hs = slice(h * HEAD_DIM, (h + 1) * HEAD_DIM)
        for c in range(tq // LANES):
            cs = slice(c * LANES, (c + 1) * LANES)
            s = (jnp.dot(k_ref[:, hs], qT_ref[hs, cs], preferred_element_type=f32)
                 + bias_ref[h, 0, :, cs] + msk_sc[:, cs])
            m_old = m_sc[h:h + 1, cs]
            m_new = jnp.maximum(m_old, jnp.max(s, axis=0, keepdims=True))
            p = jnp.exp(s - m_new)
            alpha = jnp.exp(m_old - m_new)
            l_sc[h:h + 1, cs] = alpha * l_sc[h:h + 1, cs] + jnp.sum(p, axis=0, keepdims=True)
            acc_sc[hs, cs] = alpha * acc_sc[hs, cs] + jnp.dot(vT_ref[hs, :], p.astype(bf16),
                                                              preferred_element_type=f32)
            m_sc[h:h + 1, cs] = m_new

    @pl.when(kb == qb)
    def _():
        for h in range(N_HEADS):
            hs = slice(h * HEAD_DIM, (h + 1) * HEAD_DIM)
            o_ref[:, hs] = jnp.transpose(acc_sc[hs, :] / l_sc[h:h + 1, :])


def _folded_blocks(i, j, nq):
    first = j <= i
    return jnp.where(first, i, nq - 1 - i), jnp.where(first, j, j - i - 1)


def _prompt_attention(qT, vT, qiT, wiT, ki, k, bias_tiles, B, T, tq):
    nq = T // tq
    assert nq % 2 == 0
    k_top = min(TOPK_MAX, T // 4)
    d_attn = N_HEADS * HEAD_DIM
    kern = functools.partial(_pattn_kernel, k_top=k_top, tq=tq, idx_bits=max(1, (T - 1).bit_length()))
    qblk = lambda i, j: _folded_blocks(i, j, nq)[0]
    kblk = lambda i, j: _folded_blocks(i, j, nq)[1]
    col = lambda b, i, j: (0, b * nq + qblk(i, j))
    return pl.pallas_call(
        kern, grid=(B, nq // 2, nq + 1),
        in_specs=[
            pl.BlockSpec((IDX_HEADS * IDX_DIM, tq), col),
            pl.BlockSpec((IDX_HEADS, tq), col),
            pl.BlockSpec((1, T, IDX_DIM), lambda b, i, j: (b, 0, 0)),
            pl.BlockSpec((d_attn, tq), col),
            pl.BlockSpec((tq, d_attn), lambda b, i, j: (b * nq + kblk(i, j), 0)),
            pl.BlockSpec((d_attn, tq), lambda b, i, j: (0, b * nq + kblk(i, j))),
            pl.BlockSpec((N_HEADS, 1, tq, tq), lambda b, i, j: (0, jnp.minimum(qblk(i, j) - kblk(i, j), 2), 0, 0)),
        ],
        out_specs=pl.BlockSpec((tq, d_attn), lambda b, i, j: (b * nq + qblk(i, j), 0)),
        out_shape=jax.ShapeDtypeStruct((B * T, d_attn), f32),
        scratch_shapes=[pltpu.VMEM((T, tq), i32), pltpu.VMEM((1, tq), i32), pltpu.VMEM((tq, tq), f32),
                        pltpu.VMEM((N_HEADS, tq), f32), pltpu.VMEM((N_HEADS, tq), f32),
                        pltpu.VMEM((d_attn, tq), f32)],
        compiler_params=_cparams(("parallel", "arbitrary", "arbitrary")), name="prompt_attn",
    )(qiT, wiT, ki, qT, k, vT, bias_tiles)


def _bias_lookup_kernel(rb_ref, bkt_ref, o_ref):
    bkt = bkt_ref[0]
    for h in range(N_HEADS):
        acc = jnp.zeros(bkt.shape, f32)
        for b in range(NUM_BUCKETS):
            acc = jnp.where(bkt == b, rb_ref[b, h], acc)
        o_ref[h, 0] = acc


def _prompt_bias_tiles(rel_bias, tq):
    s = jnp.arange(tq, dtype=i32)[:, None]
    t = jnp.arange(tq, dtype=i32)[None, :]
    bkt = jnp.stack([_rel_bucket(d * tq + t - s) for d in range(3)]).astype(i32)
    return pl.pallas_call(
        _bias_lookup_kernel, grid=(3,),
        in_specs=[pl.BlockSpec(memory_space=pltpu.SMEM), pl.BlockSpec((1, tq, tq), lambda d: (d, 0, 0))],
        out_specs=pl.BlockSpec((N_HEADS, 1, tq, tq), lambda d: (0, d, 0, 0)),
        out_shape=jax.ShapeDtypeStruct((N_HEADS, 3, tq, tq), f32),
        compiler_params=_cparams(("parallel",)), name="bias_tiles")(rel_bias.astype(f32), bkt)


def _ssm_tables(lam_re, lam_im, log_step, b_re, b_im, c_re, c_im, d, L):
    hp = lax.Precision.HIGHEST
    G, P = lam_re.shape
    H = b_re.shape[-1]
    lr, li = lam_re.astype(f32), lam_im.astype(f32)
    step = jnp.exp(log_step.astype(f32))[:, None]
    taus = jnp.arange(L + 1, dtype=f32)[:, None, None]
    mag = jnp.exp(lr * step * taus)
    pr, pi = mag * jnp.cos(li * step * taus), mag * jnp.sin(li * step * taus)
    x, y, den = pr[1] - 1.0, pi[1], lr * lr + li * li
    fr, fi = (x * lr + y * li) / den, (y * lr - x * li) / den
    bbr = fr[..., None] * b_re - fi[..., None] * b_im
    bbi = fr[..., None] * b_im + fi[..., None] * b_re
    clr = c_re[None] * pr[:, :, None, :] - c_im[None] * pi[:, :, None, :]
    cli = c_re[None] * pi[:, :, None, :] + c_im[None] * pr[:, :, None, :]
    kern = (jnp.einsum('tghp,gpk->tghk', clr[:L], bbr, precision=hp)
            - jnp.einsum('tghp,gpk->tghk', cli[:L], bbi, precision=hp))
    kern = kern.at[0].add(jnp.eye(H, dtype=f32)[None] * d.astype(f32)[:, :, None])
    GB = LANES // H
    J = G // GB
    rev = jnp.arange(L - 1, -1, -1)
    wr = pr[rev][:, :, :, None] * bbr[None] - pi[rev][:, :, :, None] * bbi[None]
    wi = pr[rev][:, :, :, None] * bbi[None] + pi[rev][:, :, :, None] * bbr[None]
    ka = jnp.transpose(kern.reshape(L, J, GB, H, H), (1, 0, 2, 4, 3)).reshape(J, L, LANES, H)
    to_wa = lambda a: jnp.transpose(a.reshape(L, J, GB, P, H), (1, 0, 2, 4, 3)).reshape(J, L, LANES, P)
    wa = jnp.stack([to_wa(wr), to_wa(wi)], axis=1)
    to_va = lambda a: jnp.transpose(a.reshape(L, J, GB, H, P), (1, 0, 2, 4, 3)).reshape(J, L, GB * P, H)
    va = jnp.stack([to_va(clr[1:]), -to_va(cli[1:])], axis=1)
    mT, w, v = pl.pallas_call(
        functools.partial(_ssm_expand_kernel, L=L), grid=(J,),
        in_specs=[pl.BlockSpec((1, L, LANES, H), lambda j: (j, 0, 0, 0)),
                  pl.BlockSpec((1, 2, L, LANES, P), lambda j: (j, 0, 0, 0, 0)),
                  pl.BlockSpec((1, 2, L, GB * P, H), lambda j: (j, 0, 0, 0, 0))],
        out_specs=[pl.BlockSpec((1, L * LANES, L * LANES), lambda j: (j, 0, 0)),
                   pl.BlockSpec((1, L * LANES, 2 * GB * P), lambda j: (j, 0, 0)),
                   pl.BlockSpec((1, 2 * GB * P, L * LANES), lambda j: (j, 0, 0))],
        out_shape=[jax.ShapeDtypeStruct((J, L * LANES, L * LANES), bf16),
                   jax.ShapeDtypeStruct((J, L * LANES, 2 * GB * P), bf16),
                   jax.ShapeDtypeStruct((J, 2 * GB * P, L * LANES), bf16)],
        compiler_params=_cparams(("parallel",)), name="ssm_tables")(ka, wa, va)
    lam = jnp.concatenate([pr[L].reshape(J, 1, GB * P), pi[L].reshape(J, 1, GB * P)], axis=2)
    return mT, w, v, lam


def _ssm_expand_kernel(ka_ref, wa_ref, va_ref, mt_ref, w_ref, v_ref, *, L):
    H = ka_ref.shape[3]
    P = wa_ref.shape[4]
    S = v_ref.shape[1] // 2

    def repeat(n, width):
        return jnp.where(lax.broadcasted_iota(i32, (n, width), 1) % n == lax.broadcasted_iota(i32, (n, width), 0),
                         1.0, 0.0).astype(bf16)

    def same_group(rows, rdiv, cols, cdiv):
        return (lax.broadcasted_iota(i32, (rows, cols), 0) // rdiv) == (lax.broadcasted_iota(i32, (rows, cols), 1) // cdiv)

    rep_h, rep_p = repeat(H, LANES), repeat(P, S)
    m_mask, w_mask, v_mask = same_group(LANES, H, LANES, H), same_group(LANES, H, S, P), same_group(S, P, LANES, H)
    zero = jnp.zeros((LANES, LANES), bf16)
    for tau in range(L):
        tile = jnp.where(m_mask, jnp.dot(ka_ref[0, tau].astype(bf16), rep_h, preferred_element_type=f32), 0.0).astype(bf16)
        for s in range(L - tau):
            mt_ref[0, s * LANES:(s + 1) * LANES, (s + tau) * LANES:(s + tau + 1) * LANES] = tile
    for s in range(L):
        for t in range(s):
            mt_ref[0, s * LANES:(s + 1) * LANES, t * LANES:(t + 1) * LANES] = zero
    for half in range(2):
        for s in range(L):
            wt = jnp.dot(wa_ref[0, half, s].astype(bf16), rep_p, preferred_element_type=f32)
            w_ref[0, s * LANES:(s + 1) * LANES, half * S:(half + 1) * S] = jnp.where(w_mask, wt, 0.0).astype(bf16)
            vt = jnp.dot(va_ref[0, half, s].astype(bf16), rep_h, preferred_element_type=f32)
            v_ref[0, half * S:(half + 1) * S, s * LANES:(s + 1) * LANES] = jnp.where(v_mask, vt, 0.0).astype(bf16)


def _ssm_kernel(u_ref, mt_ref, w_ref, v_ref, lam_ref, h0_ref, y_ref, hout_ref, uc_sc, s_sc, hin_sc, yc_sc,
                *, L, rows, seq):
    S = s_sc.shape[1] // 2
    for s in range(L):
        uc_sc[:, s * LANES:(s + 1) * LANES] = u_ref[pl.ds(s, rows, stride=L), :].astype(bf16)
    uc = uc_sc[...]
    s_sc[...] = jnp.dot(uc, w_ref[0], preferred_element_type=f32)
    lr, li = lam_ref[0, :, 0:S], lam_ref[0, :, S:2 * S]
    if seq:
        def body(c, carry):
            hr, hi = carry
            hin_sc[pl.ds(c, 1), 0:S] = hr
            hin_sc[pl.ds(c, 1), S:2 * S] = hi
            sr, si = s_sc[pl.ds(c, 1), 0:S], s_sc[pl.ds(c, 1), S:2 * S]
            return lr * hr - li * hi + sr, lr * hi + li * hr + si

        hr, hi = lax.fori_loop(0, rows, body, (h0_ref[0, 0, :, 0:S], h0_ref[0, 0, :, S:2 * S]))
    else:
        hr0, hi0 = h0_ref[0, 0, :, 0:S], h0_ref[0, 0, :, S:2 * S]
        hin_sc[...] = h0_ref[0, 0]
        hr = lr * hr0 - li * hi0 + s_sc[:, 0:S]
        hi = lr * hi0 + li * hr0 + s_sc[:, S:2 * S]
    hout_ref[0, 0, :, 0:S] = hr
    hout_ref[0, 0, :, S:2 * S] = hi
    yc_sc[...] = (jnp.dot(uc, mt_ref[0], preferred_element_type=f32)
                  + jnp.dot(hin_sc[...].astype(bf16), v_ref[0], preferred_element_type=f32))
    for t in range(L):
        y_ref[pl.ds(t, rows, stride=L), :] = yc_sc[:, t * LANES:(t + 1) * LANES]


def _ssm(zn, u_blk0, d_ssm, n_steps, h0, tables, L, seq):
    mT, w, v, lam = tables
    N = zn.shape[0]
    J = d_ssm // LANES
    tok = N // n_steps
    rows = tok // L
    rows_h = h0.shape[2]
    S2 = h0.shape[3]
    kern = functools.partial(_ssm_kernel, L=L, rows=rows, seq=seq)
    return pl.pallas_call(
        kern, grid=(J, n_steps),
        in_specs=[
            pl.BlockSpec((tok, LANES), lambda j, b: (b, u_blk0 + j)),
            pl.BlockSpec((1, L * LANES, L * LANES), lambda j, b: (j, 0, 0)),
            pl.BlockSpec((1, L * LANES, S2), lambda j, b: (j, 0, 0)),
            pl.BlockSpec((1, S2, L * LANES), lambda j, b: (j, 0, 0)),
            pl.BlockSpec((1, 1, S2), lambda j, b: (j, 0, 0)),
            pl.BlockSpec((1, 1, rows_h, S2), lambda j, b: (j, b, 0, 0)),
        ],
        out_specs=[pl.BlockSpec((tok, LANES), lambda j, b: (b, j)),
                   pl.BlockSpec((1, 1, rows_h, S2), lambda j, b: (j, b, 0, 0))],
        out_shape=[jax.ShapeDtypeStruct((N, d_ssm), f32),
                   jax.ShapeDtypeStruct(h0.shape, f32)],
        scratch_shapes=[pltpu.VMEM((rows, L * LANES), bf16), pltpu.VMEM((rows, S2), f32),
                        pltpu.VMEM((rows, S2), f32), pltpu.VMEM((rows, L * LANES), f32)],
        compiler_params=_cparams(("parallel", "arbitrary")), name="ssm",
    )(zn, mT, w, v, lam, h0)


def _pack_state(a_re, a_im):
    Bn, G, P = a_re.shape
    J = G * SSM_GROUP // LANES
    blk = lambda a: jnp.transpose(a.astype(f32).reshape(Bn, J, (G // J) * P), (1, 0, 2))
    return jnp.concatenate([blk(a_re), blk(a_im)], axis=2)


def _unpack_state(h, G, P):
    J, Bn, S2 = h.shape
    un = lambda a: jnp.transpose(a, (1, 0, 2)).reshape(Bn, G, P)
    return un(h[:, :, :S2 // 2]), un(h[:, :, S2 // 2:])


def _sigmoid(x):
    return 1.0 / (1.0 + jnp.exp(-x))


def _gelu_tanh(x):
    return 0.5 * x * (1.0 + jnp.tanh(math.sqrt(2.0 / math.pi) * (x + 0.044715 * (x * x * x))))


def _layer_norm(r, g, b):
    mu = jnp.mean(r, axis=-1, keepdims=True)
    c = r - mu
    var = jnp.mean(c * c, axis=-1, keepdims=True)
    return c * lax.rsqrt(var + LN_EPS) * g + b


def _merge_kernel(x_ref, attn_ref, y_ref, ga_ref, gb_ref, wa_ref, wg_ref, ws_ref, wo_ref, g_ref, b_ref, o_ref, or_ref):
    a = jnp.dot(attn_ref[...].astype(bf16), wa_ref[...], preferred_element_type=f32)
    z = _gelu_tanh(y_ref[...])
    gl = jnp.dot(z.astype(bf16), wg_ref[...], preferred_element_type=f32)
    s = jnp.dot((z * _sigmoid(gl)).astype(bf16), ws_ref[...], preferred_element_type=f32)
    m = _sigmoid(ga_ref[...]) * a + _sigmoid(gb_ref[...]) * s
    mix = jnp.dot(m.astype(bf16), wo_ref[...], preferred_element_type=f32)
    o_ref[...] = _layer_norm(ALPHA * x_ref[...] + mix, g_ref[...], b_ref[...])
    _store_token_rows(or_ref, o_ref)


def _store_token_rows(rows_ref, src_ref):
    n, D = src_ref.shape
    for c in range(D // LANES):
        rows_ref[pl.ds(c, n, stride=D // LANES), :] = src_ref[:, c * LANES:(c + 1) * LANES]


def _load_token_rows(dst_ref, rows_ref, n):
    D = dst_ref.shape[1]
    for c in range(D // LANES):
        dst_ref[:, c * LANES:(c + 1) * LANES] = rows_ref[pl.ds(c, n, stride=D // LANES), :].astype(dst_ref.dtype)


def _merge(x, attn, ssm_y, zn, ga_blk, gb_blk, wa, wg, ws, wo, g, b, tm):
    N, D = x.shape
    Dh = attn.shape[1]
    const = lambda shape: pl.BlockSpec(shape, lambda i: (0, 0), pipeline_mode=pl.Buffered(1))
    return pl.pallas_call(
        _merge_kernel, grid=(N // tm,),
        in_specs=[pl.BlockSpec((tm, D), lambda i: (i, 0)),
                  pl.BlockSpec((tm, Dh), lambda i: (i, 0)),
                  pl.BlockSpec((tm, Dh), lambda i: (i, 0)),
                  pl.BlockSpec((tm, D), lambda i: (i, ga_blk)),
                  pl.BlockSpec((tm, D), lambda i: (i, gb_blk)),
                  const((Dh, D)), const((Dh, Dh)), const((Dh, D)), const((D, D)),
                  const((1, D)), const((1, D))],
        out_specs=[pl.BlockSpec((tm, D), lambda i: (i, 0)),
                   pl.BlockSpec((tm * (D // LANES), LANES), lambda i: (i, 0))],
        out_shape=[jax.ShapeDtypeStruct((N, D), f32), jax.ShapeDtypeStruct((N * (D // LANES), LANES), f32)],
        compiler_params=_cparams(("parallel",)), name="merge_ln1",
    )(x, attn, ssm_y, zn, zn, wa, wg, ws, wo, g, b)


ROUTE_E1, ROUTE_E2, ROUTE_W1, ROUTE_W2, ROUTE_R1, ROUTE_R2 = range(6)


def _router_kernel(x_ref, w_ref, b_ref, o_ref, route_ref, cnt_ref, run_sc):
    E = N_EXPERT_GROUPS * EXPERTS_PER_GROUP

    @pl.when(pl.program_id(0) == 0)
    def _():
        run_sc[...] = jnp.zeros(run_sc.shape, f32)

    logit = jnp.dot(x_ref[...], w_ref[...], preferred_element_type=f32, precision=lax.Precision.HIGHEST) + b_ref[...]
    lane = lax.broadcasted_iota(i32, logit.shape, 1)
    is_g = (lane >= E) & (lane < E + N_EXPERT_GROUPS)
    glog = jnp.where(is_g, logit, -jnp.inf)
    g_max = jnp.max(glog, axis=1, keepdims=True)
    g_idx = jnp.min(jnp.where(glog == g_max, lane, 4 * LANES), axis=1, keepdims=True) - E
    g_w = 1.0 / jnp.sum(jnp.where(is_g, jnp.exp(glog - g_max), 0.0), axis=1, keepdims=True)
    in_g = (lane < E) & (lane // EXPERTS_PER_GROUP == g_idx)
    e1 = jnp.where(in_g, logit, -jnp.inf)
    v1 = jnp.max(e1, axis=1, keepdims=True)
    i1 = jnp.min(jnp.where(e1 == v1, lane, 4 * LANES), axis=1, keepdims=True)
    e2 = jnp.where(lane == i1, -jnp.inf, e1)
    v2 = jnp.max(e2, axis=1, keepdims=True)
    i2 = jnp.min(jnp.where(e2 == v2, lane, 4 * LANES), axis=1, keepdims=True)
    t = jnp.exp(v2 - v1)
    w1 = g_w / (1.0 + t)
    w2 = g_w * t / (1.0 + t)
    o_ref[...] = jnp.where(lane == i1, w1, 0.0) + jnp.where(lane == i2, w2, 0.0)
    tm = logit.shape[0]
    hit = jnp.where((lane == i1) | (lane == i2), 1.0, 0.0)
    before = jnp.where(lax.broadcasted_iota(i32, (tm, tm), 0) > lax.broadcasted_iota(i32, (tm, tm), 1), 1.0, 0.0)
    rank = jnp.dot(before.astype(bf16), hit.astype(bf16), preferred_element_type=f32) + run_sc[...]
    r1 = jnp.sum(jnp.where(lane == i1, rank, 0.0), axis=1, keepdims=True)
    r2 = jnp.sum(jnp.where(lane == i2, rank, 0.0), axis=1, keepdims=True)
    rec = jnp.zeros(logit.shape, f32)
    for pos, val in ((ROUTE_E1, i1.astype(f32)), (ROUTE_E2, i2.astype(f32)), (ROUTE_W1, w1), (ROUTE_W2, w2),
                     (ROUTE_R1, r1), (ROUTE_R2, r2)):
        rec = jnp.where(lane == pos, val, rec)
    route_ref[...] = rec
    run_sc[...] += jnp.sum(hit, axis=0, keepdims=True)
    cnt_ref[...] = run_sc[...]


def _router(x1, w_r, b_r, tm):
    N, D = x1.shape
    tile = pl.BlockSpec((tm, LANES), lambda i: (i, 0))
    return pl.pallas_call(
        _router_kernel, grid=(N // tm,),
        in_specs=[pl.BlockSpec((tm, D), lambda i: (i, 0)),
                  pl.BlockSpec((D, LANES), lambda i: (0, 0)),
                  pl.BlockSpec((1, LANES), lambda i: (0, 0))],
        out_specs=[tile, tile, pl.BlockSpec((1, LANES), lambda i: (0, 0))],
        out_shape=[jax.ShapeDtypeStruct((N, LANES), f32), jax.ShapeDtypeStruct((N, LANES), f32),
                   jax.ShapeDtypeStruct((1, LANES), f32)],
        scratch_shapes=[pltpu.VMEM((1, LANES), f32)],
        compiler_params=_cparams(("arbitrary",)), name="router",
    )(x1, w_r, b_r)


def _moe_routed_kernel(te_ref, nu_ref, inv_ref, x_hbm, wg_ref, wu_ref, wd_ref, o_ref, xbuf, xb_sc, y_sc, sem, *, R, C):
    t = pl.program_id(0)
    n_used = nu_ref[0]
    slot = t % 2

    def row_copy(tt, sl, j):
        tok = inv_ref[tt * R + j]
        return pltpu.make_async_copy(x_hbm.at[pl.ds(tok * C, C), :], xbuf.at[sl, pl.ds(j * C, C), :], sem.at[sl])

    def issue(tt, sl):
        def body(j, _):
            row_copy(tt, sl, j).start()
            return 0
        lax.fori_loop(0, R, body, 0, unroll=DMA_UNROLL)

    @pl.when(t == 0)
    def _():
        issue(0, 0)

    @pl.when(t + 1 < n_used)
    def _():
        issue(t + 1, 1 - slot)

    @pl.when(t < n_used)
    def _():
        def wait_body(j, _):
            row_copy(t, slot, j).wait()
            return 0
        lax.fori_loop(0, R, wait_body, 0, unroll=DMA_UNROLL)
        _load_token_rows(xb_sc, xbuf.at[slot], R)
        xb = xb_sc[...]
        hg = jnp.dot(xb, wg_ref[0], preferred_element_type=f32)
        hu = jnp.dot(xb, wu_ref[0], preferred_element_type=f32)
        h = hg * _sigmoid(hg) * hu
        y_sc[...] = jnp.dot(h.astype(bf16), wd_ref[0], preferred_element_type=f32)
        _store_token_rows(o_ref, y_sc)

    @pl.when(t >= n_used)
    def _():
        o_ref[...] = jnp.zeros(o_ref.shape, f32)


def _moe_combine_kernel(p1_ref, p2_ref, x_ref, route_ref, g_ref, b_ref, y_hbm, o_ref, ybuf, r_sc, sem, *, tm, C):
    i = pl.program_id(0)
    n = pl.num_programs(0)
    slot = i % 2

    def row_copies(ii, sl, j):
        a = pltpu.make_async_copy(y_hbm.at[pl.ds(p1_ref[ii * tm + j] * C, C), :], ybuf.at[sl, 0, pl.ds(j * C, C), :], sem.at[sl])
        b = pltpu.make_async_copy(y_hbm.at[pl.ds(p2_ref[ii * tm + j] * C, C), :], ybuf.at[sl, 1, pl.ds(j * C, C), :], sem.at[sl])
        return a, b

    def issue(ii, sl):
        def body(j, _):
            a, b = row_copies(ii, sl, j)
            a.start()
            b.start()
            return 0
        lax.fori_loop(0, tm, body, 0, unroll=DMA_UNROLL)

    @pl.when(i == 0)
    def _():
        issue(0, 0)

    @pl.when(i + 1 < n)
    def _():
        issue(i + 1, 1 - slot)

    def wait_body(j, _):
        a, b = row_copies(i, slot, j)
        a.wait()
        b.wait()
        return 0

    lax.fori_loop(0, tm, wait_body, 0, unroll=DMA_UNROLL)
    route = route_ref[...]
    lane = lax.broadcasted_iota(i32, route.shape, 1)
    w1 = jnp.sum(jnp.where(lane == ROUTE_W1, route, 0.0), axis=1, keepdims=True)
    w2 = jnp.sum(jnp.where(lane == ROUTE_W2, route, 0.0), axis=1, keepdims=True)
    for c in range(C):
        cs = slice(c * LANES, (c + 1) * LANES)
        y1 = ybuf[slot, 0, pl.ds(c, tm, stride=C), :]
        y2 = ybuf[slot, 1, pl.ds(c, tm, stride=C), :]
        r_sc[:, cs] = ALPHA * x_ref[:, cs] + (w1 * y1 + w2 * y2)
    o_ref[...] = _layer_norm(r_sc[...], g_ref[...], b_ref[...])


def _moe_routed(x1, x1_rows, route, counts, wg, wu, wd, g, b, R, tm):
    N, D = x1.shape
    C = D // LANES
    E, _, De = wg.shape
    n_tiles = (2 * N + E * (R - 1)) // R + 1
    col = lambda k: route[:, k]
    e1, e2 = col(ROUTE_E1).astype(i32), col(ROUTE_E2).astype(i32)
    cnt = counts[0, :E].astype(i32)
    padded = (cnt + R - 1) // R * R
    seg_end = jnp.cumsum(padded)
    seg_start = seg_end - padded
    pos1 = seg_start[e1] + col(ROUTE_R1).astype(i32)
    pos2 = seg_start[e2] + col(ROUTE_R2).astype(i32)
    tile_start = jnp.arange(n_tiles, dtype=i32) * R
    tile_expert = jnp.minimum(jnp.sum((tile_start[:, None] >= seg_end[None, :]).astype(i32), axis=1), E - 1)
    n_used = (seg_end[E - 1] // R).reshape(1)
    tok = jnp.arange(N, dtype=i32)
    inv = jnp.zeros((n_tiles * R,), i32).at[pos1].set(tok).at[pos2].set(tok)
    wspec = lambda shape: pl.BlockSpec(shape, lambda t, te, nu, iv: (te[t], 0, 0))
    ys = pl.pallas_call(
        functools.partial(_moe_routed_kernel, R=R, C=C),
        grid_spec=pltpu.PrefetchScalarGridSpec(
            num_scalar_prefetch=3, grid=(n_tiles,),
            in_specs=[pl.BlockSpec(memory_space=pl.ANY), wspec((1, D, De)), wspec((1, D, De)), wspec((1, De, D))],
            out_specs=pl.BlockSpec((R * C, LANES), lambda t, te, nu, iv: (t, 0)),
            scratch_shapes=[pltpu.VMEM((2, R * C, LANES), f32), pltpu.VMEM((R, D), bf16), pltpu.VMEM((R, D), f32),
                            pltpu.SemaphoreType.DMA((2,))]),
        out_shape=jax.ShapeDtypeStruct((n_tiles * R * C, LANES), f32),
        compiler_params=pltpu.CompilerParams(dimension_semantics=("arbitrary",), vmem_limit_bytes=VMEM_LIMIT,
                                             disable_bounds_checks=True),
        name="moe_experts",
    )(tile_expert, n_used, inv, x1_rows, wg, wu, wd)
    return pl.pallas_call(
        functools.partial(_moe_combine_kernel, tm=tm, C=C),
        grid_spec=pltpu.PrefetchScalarGridSpec(
            num_scalar_prefetch=2, grid=(N // tm,),
            in_specs=[pl.BlockSpec((tm, D), lambda i, p1, p2: (i, 0)),
                      pl.BlockSpec((tm, LANES), lambda i, p1, p2: (i, 0)),
                      pl.BlockSpec((1, D), lambda i, p1, p2: (0, 0)),
                      pl.BlockSpec((1, D), lambda i, p1, p2: (0, 0)),
                      pl.BlockSpec(memory_space=pl.ANY)],
            out_specs=pl.BlockSpec((tm, D), lambda i, p1, p2: (i, 0)),
            scratch_shapes=[pltpu.VMEM((2, 2, tm * C, LANES), f32), pltpu.VMEM((tm, D), f32),
                            pltpu.SemaphoreType.DMA((2,))]),
        out_shape=jax.ShapeDtypeStruct((N, D), f32),
        compiler_params=pltpu.CompilerParams(dimension_semantics=("arbitrary",), vmem_limit_bytes=VMEM_LIMIT,
                                             disable_bounds_checks=True),
        name="moe_combine_ln2",
    )(pos1, pos2, x1, route, g, b, ys)


def _moe_kernel(x_ref, gate_ref, wg_ref, wu_ref, wd_ref, g_ref, b_ref, o_ref, xb_sc, acc_sc):
    e = pl.program_id(1)

    @pl.when(e == 0)
    def _():
        xb_sc[...] = x_ref[...].astype(bf16)
        acc_sc[...] = jnp.zeros(acc_sc.shape, f32)

    gates = gate_ref[...]
    lane = lax.broadcasted_iota(i32, gates.shape, 1)
    ge = jnp.sum(jnp.where(lane == e, gates, 0.0), axis=1, keepdims=True)
    xb = xb_sc[...]
    hg = jnp.dot(xb, wg_ref[0], preferred_element_type=f32)
    hu = jnp.dot(xb, wu_ref[0], preferred_element_type=f32)
    h = hg * _sigmoid(hg) * hu
    acc_sc[...] += jnp.dot((h * ge).astype(bf16), wd_ref[0], preferred_element_type=f32)

    @pl.when(e == pl.num_programs(1) - 1)
    def _():
        o_ref[...] = _layer_norm(ALPHA * x_ref[...] + acc_sc[...], g_ref[...], b_ref[...])


def _moe(x1, gates, wg, wu, wd, g, b, tm):
    N, D = x1.shape
    E, _, De = wg.shape
    return pl.pallas_call(
        _moe_kernel, grid=(N // tm, E),
        in_specs=[pl.BlockSpec((tm, D), lambda i, e: (i, 0)),
                  pl.BlockSpec((tm, LANES), lambda i, e: (i, 0)),
                  pl.BlockSpec((1, D, De), lambda i, e: (e, 0, 0)),
                  pl.BlockSpec((1, D, De), lambda i, e: (e, 0, 0)),
                  pl.BlockSpec((1, De, D), lambda i, e: (e, 0, 0)),
                  pl.BlockSpec((1, D), lambda i, e: (0, 0)),
                  pl.BlockSpec((1, D), lambda i, e: (0, 0))],
        out_specs=pl.BlockSpec((tm, D), lambda i, e: (i, 0)),
        out_shape=jax.ShapeDtypeStruct((N, D), f32),
        scratch_shapes=[pltpu.VMEM((tm, D), bf16), pltpu.VMEM((tm, D), f32)],
        compiler_params=_cparams(("parallel", "arbitrary")), name="moe_ln2",
    )(x1, gates, wg, wu, wd, g, b)


def _sscore_kernel(pt_ref, qi_ref, wi_ref, kn_ref, *rest, pp):
    pages, (o_ref, on_ref) = rest[:pp], rest[pp:]
    qi = qi_ref[0]
    wi = wi_ref[0]
    tn = qi.shape[0] // IDX_HEADS

    def scores(keys):
        s = lax.dot_general(qi, keys, (((1,), (1,)), ((), ())), preferred_element_type=f32)
        r = jnp.maximum(s, 0.0) * wi
        return jnp.sum(r.reshape(tn, IDX_HEADS, keys.shape[0]), axis=1)

    for j in range(pp):
        o_ref[0, :, j * LANES:(j + 1) * LANES] = scores(pages[j][0].astype(bf16))

    @pl.when(pl.program_id(1) == 0)
    def _():
        on_ref[0] = scores(kn_ref[0])


def _sample_scores(page_table, qi, wi, ki_new, cache_kidx, pp):
    Bd, Tn = qi.shape[:2]
    n_pages = page_table.shape[1]
    ps = cache_kidx.shape[1]
    rows = Tn * IDX_HEADS
    qi2 = qi.reshape(Bd, rows, IDX_DIM).astype(bf16)
    wi2 = wi.reshape(Bd, rows, 1)
    kn = jnp.pad(ki_new, ((0, 0), (0, LANES - Tn), (0, 0))).astype(bf16)
    page_spec = lambda j: pl.BlockSpec((1, ps, IDX_DIM), lambda b, p, pt: (pt[b, p * pp + j], 0, 0))
    grid_spec = pltpu.PrefetchScalarGridSpec(
        num_scalar_prefetch=1, grid=(Bd, n_pages // pp),
        in_specs=[pl.BlockSpec((1, rows, IDX_DIM), lambda b, p, pt: (b, 0, 0)),
                  pl.BlockSpec((1, rows, 1), lambda b, p, pt: (b, 0, 0)),
                  pl.BlockSpec((1, LANES, IDX_DIM), lambda b, p, pt: (b, 0, 0))]
                 + [page_spec(j) for j in range(pp)],
        out_specs=[pl.BlockSpec((1, Tn, pp * ps), lambda b, p, pt: (b, 0, p)),
                   pl.BlockSpec((1, Tn, LANES), lambda b, p, pt: (b, 0, 0))])
    return pl.pallas_call(
        functools.partial(_sscore_kernel, pp=pp), grid_spec=grid_spec,
        out_shape=[jax.ShapeDtypeStruct((Bd, Tn, n_pages * ps), f32),
                   jax.ShapeDtypeStruct((Bd, Tn, LANES), f32)],
        compiler_params=_cparams(("parallel", "arbitrary")), name="sample_scores",
    )(page_table, qi2, wi2, kn, *([cache_kidx] * pp))


def _sselect_kernel(sp_ref, sn_ref, o_ref, keys_sc, *, k_top, tn, idx_bits):
    rows = sp_ref.shape[0]
    past = sp_ref.shape[1]
    n_tiles = past // LANES

    def load(j, _):
        c0 = pl.multiple_of(j * LANES, LANES)
        keys_sc[pl.ds(c0, LANES), :] = _order_key(jnp.transpose(sp_ref[:, pl.ds(c0, LANES)]))
        return 0

    lax.fori_loop(0, n_tiles, load, 0)
    new = _order_key(jnp.transpose(sn_ref[...]))
    jj = lax.broadcasted_iota(i32, new.shape, 0)
    qq = lax.broadcasted_iota(i32, new.shape, 1) % tn
    keys_sc[pl.ds(past, LANES), :] = jnp.where(jj <= qq, new, INT_MIN)
    thr = _select_rows(keys_sc, past + LANES, k_top, idx_bits, LANES)

    none = np.int32(2 ** 30)
    o_ref[...] = jnp.zeros(o_ref.shape, i32)
    slot = lax.broadcasted_iota(i32, o_ref.shape, 0)

    def slab(j, filled):
        c0 = pl.multiple_of(j * LANES, LANES)
        sel = keys_sc[pl.ds(c0, LANES), :] >= thr
        cand = jnp.where(sel, c0 + lax.broadcasted_iota(i32, sel.shape, 0), none)
        n_sel = jnp.sum(jnp.where(sel, 1, 0).astype(i32), axis=0, keepdims=True)

        def extract(i, cand):
            first = jnp.min(cand, axis=0, keepdims=True)
            o_ref[...] = jnp.where((slot == filled + i) & (first < none), first, o_ref[...])
            return jnp.where(cand == first, none, cand)

        lax.fori_loop(0, jnp.max(n_sel), extract, cand)
        return filled + n_sel

    lax.fori_loop(0, n_tiles + 1, slab, jnp.zeros((1, rows), i32))


def _sample_select(scores_past, scores_new, tn):
    assert scores_past.shape[0] <= LANES
    pad_rows = lambda a: jnp.pad(a, ((0, LANES - a.shape[0]), (0, 0)))
    scores_past, scores_new = pad_rows(scores_past), pad_rows(scores_new)
    R, past = scores_past.shape
    k_top = min(TOPK_MAX, (past + tn) // 4)
    assert past + 1 >= k_top
    kern = functools.partial(_sselect_kernel, k_top=k_top, tn=tn, idx_bits=(past + LANES - 1).bit_length())
    return pl.pallas_call(
        kern, grid=(1,),
        in_specs=[pl.BlockSpec((R, past), lambda i: (0, 0)), pl.BlockSpec((R, LANES), lambda i: (0, 0))],
        out_specs=pl.BlockSpec((k_top, R), lambda i: (0, 0)),
        out_shape=jax.ShapeDtypeStruct((k_top, R), i32),
        scratch_shapes=[pltpu.VMEM((past + LANES, R), i32)],
        compiler_params=_cparams(("arbitrary",)), name="sample_select",
    )(scores_past, scores_new)


def _sgather_kernel(idx_ref, row_ref, qT_ref, idxv_ref, knew_ref, vnew_ref, rb_ref, ck_hbm, cv_hbm, o_ref,
                    kbuf, vbuf, ksem, vsem, *, K, tn, past):
    r = pl.program_id(0)
    n = pl.num_programs(0)
    slot = r % 2

    def row_copies(rr, sl, j):
        row = row_ref[rr * K + j]
        return (pltpu.make_async_copy(ck_hbm.at[row], kbuf.at[sl, j], ksem.at[sl]),
                pltpu.make_async_copy(cv_hbm.at[row], vbuf.at[sl, j], vsem.at[sl]))

    def issue(rr, sl):
        def body(j, _):
            ck, cv = row_copies(rr, sl, j)
            ck.start()
            cv.start()
            return 0
        lax.fori_loop(0, K, body, 0, unroll=DMA_UNROLL)

    @pl.when(r == 0)
    def _():
        issue(0, 0)

    @pl.when(r + 1 < n)
    def _():
        issue(r + 1, 1 - slot)

    def wait_body(j, _):
        ck, cv = row_copies(r, slot, j)
        ck.wait()
        cv.wait()
        return 0

    lax.fori_loop(0, K, wait_body, 0, unroll=DMA_UNROLL)

    for t in range(tn):
        j = K - 1 - t
        pos = idx_ref[r * K + j]

        @pl.when(pos >= past)
        def _():
            kbuf[slot, j] = knew_ref[0, pos - past]
            vbuf[slot, j] = vnew_ref[0, pos - past]

    idxv = idxv_ref[0]
    k = kbuf[slot]
    v = vbuf[slot]
    pairs = jnp.dot(k.reshape(K * N_HEADS, HEAD_DIM).astype(bf16), qT_ref[0], preferred_element_type=f32)
    pairs = pairs.reshape(K, N_HEADS, LANES)
    diag = lax.broadcasted_iota(i32, pairs.shape, 1) == lax.broadcasted_iota(i32, pairs.shape, 2)
    logits = jnp.sum(jnp.where(diag, pairs, 0.0), axis=1)
    dist = past + r % tn - idxv
    onehot = jnp.where(_rel_bucket(dist) == lax.broadcasted_iota(i32, (K, LANES), 1), 1.0, 0.0)
    logits = logits + jnp.dot(onehot, rb_ref[...], preferred_element_type=f32, precision=lax.Precision.HIGHEST)
    p = jnp.exp(logits - jnp.max(logits, axis=0, keepdims=True))
    p = p / jnp.sum(p, axis=0, keepdims=True)
    pd = jnp.where(diag, jnp.broadcast_to(p[:, None, :], pairs.shape), 0.0).reshape(K * N_HEADS, LANES)
    spread = jnp.dot(pd.astype(bf16), jnp.ones((LANES, HEAD_DIM), bf16), preferred_element_type=f32)
    o_ref[0] = jnp.sum(spread.reshape(K, N_HEADS, HEAD_DIM) * v, axis=0)


def _sample_attention(page_table, q, k_new, v_new, idx_list, rel_bias, cache_k, cache_v):
    Bd, Tn = q.shape[:2]
    n_pages = page_table.shape[1]
    ps = cache_k.shape[1]
    past = n_pages * ps
    R = Bd * Tn
    K = idx_list.shape[0]
    idx = jnp.transpose(idx_list)[:R]
    past_idx = jnp.minimum(idx, past - 1).reshape(Bd, Tn * K)
    rows = jnp.take_along_axis(page_table, past_idx // ps, axis=1) * ps + past_idx % ps
    n_phys = cache_k.shape[0]
    qT = jnp.transpose((q * (HEAD_DIM ** -0.5)).reshape(R, N_HEADS, HEAD_DIM), (0, 2, 1))
    qT = jnp.pad(qT, ((0, 0), (0, 0), (0, LANES - N_HEADS))).astype(bf16)
    rb = jnp.pad(rel_bias.astype(f32), ((0, LANES - NUM_BUCKETS), (0, LANES - N_HEADS)))
    row = lambda r, idx_s, pt_s: (r, 0, 0)
    batch = lambda r, idx_s, pt_s: (r // Tn, 0, 0, 0)
    grid_spec = pltpu.PrefetchScalarGridSpec(
        num_scalar_prefetch=2, grid=(R,),
        in_specs=[pl.BlockSpec((1, HEAD_DIM, LANES), row),
                  pl.BlockSpec((1, K, 1), row),
                  pl.BlockSpec((1, Tn, N_HEADS, HEAD_DIM), batch),
                  pl.BlockSpec((1, Tn, N_HEADS, HEAD_DIM), batch),
                  pl.BlockSpec((LANES, LANES), lambda r, idx_s, pt_s: (0, 0)),
                  pl.BlockSpec(memory_space=pl.ANY),
                  pl.BlockSpec(memory_space=pl.ANY)],
        out_specs=pl.BlockSpec((1, N_HEADS, HEAD_DIM), row),
        scratch_shapes=[pltpu.VMEM((2, K, N_HEADS, HEAD_DIM), f32), pltpu.VMEM((2, K, N_HEADS, HEAD_DIM), f32),
                        pltpu.SemaphoreType.DMA((2,)), pltpu.SemaphoreType.DMA((2,))])
    kern = functools.partial(_sgather_kernel, K=K, tn=Tn, past=past)
    flat_cache = lambda c: c.reshape(n_phys * ps, N_HEADS, HEAD_DIM)
    out = pl.pallas_call(
        kern, grid_spec=grid_spec,
        out_shape=jax.ShapeDtypeStruct((R, N_HEADS, HEAD_DIM), f32),
        compiler_params=pltpu.CompilerParams(dimension_semantics=("arbitrary",), vmem_limit_bytes=VMEM_LIMIT,
                                             disable_bounds_checks=True),
        name="sample_attn",
    )(idx.reshape(R * K), rows.reshape(R * K).astype(i32), qT, idx.reshape(R, K, 1), k_new, v_new, rb,
      flat_cache(cache_k), flat_cache(cache_v))
    return out.reshape(Bd, Tn, N_HEADS * HEAD_DIM)


def _pick(n, cands):
    for c in cands:
        if n % c == 0:
            return c
    return n


def kernel(x_prompt, x_sample, cache_k, cache_v, cache_kidx, state_ssm_re, state_ssm_im, page_table, rel_bias, w_in, ssm_lambda_re, ssm_lambda_im, ssm_log_step, ssm_b_re, ssm_b_im, ssm_c_re, ssm_c_im, ssm_d, w_glu, w_attn_out, w_ssm_out, w_o, ln1_g, ln1_b, w_group_router, b_group_router, w_expert_router, b_expert_router, w_exp_gate, w_exp_up, w_exp_down, ln2_g, ln2_b):
    B, T, D = x_prompt.shape
    Bd, Tn = x_sample.shape[:2]
    n_phys, ps = cache_k.shape[1:3]
    d_attn = N_HEADS * HEAD_DIM
    d_qi = IDX_HEADS * IDX_DIM
    d_ssm = D // 2
    G = d_ssm // SSM_GROUP
    E = N_EXPERT_GROUPS * EXPERTS_PER_GROUP
    l = 0
    w = w_in[l]
    o_q, o_k, o_v, o_qi = 0, d_attn, 2 * d_attn, 3 * d_attn
    o_ki = o_qi + d_qi
    o_wi = o_ki + IDX_DIM
    o_u = o_wi + IDX_HEADS
    o_ga = o_u + d_ssm
    o_gb = o_ga + D
    cols = lambda a, n: w[:, a:a + n]

    small = jnp.concatenate([cols(o_ki, IDX_DIM), cols(o_wi, IDX_HEADS) * INDEXER_SCALE,
                             jnp.zeros((D, LANES - IDX_DIM - IDX_HEADS), f32)], axis=1)
    w_n = jnp.concatenate([cols(o_ga, D), cols(o_gb, D), cols(o_k, d_attn), cols(o_v, d_attn), small, cols(o_u, d_ssm)],
                          axis=1)
    c_ga, c_gb, c_k = 0, D, 2 * D
    c_v = c_k + d_attn
    c_s = c_v + d_attn
    c_u = c_s + LANES
    n_cols = c_u + d_ssm
    tn_n = 768
    w_n = jnp.pad(w_n, ((0, 0), (0, _round_up(n_cols, tn_n) - n_cols))).astype(bf16)
    w_t = jnp.concatenate([cols(o_q, d_attn) * (HEAD_DIM ** -0.5), cols(o_v, d_attn), cols(o_qi, d_qi)], axis=1).T.astype(bf16)
    w_wi = jnp.pad((cols(o_wi, IDX_HEADS) * INDEXER_SCALE).T, ((0, 2 * SUBLANES - IDX_HEADS), (0, 0))).astype(bf16)

    wa, wg, ws, wo = (a[l].astype(bf16) for a in (w_attn_out, w_glu, w_ssm_out, w_o))
    w_r = jnp.concatenate([w_expert_router[l], w_group_router[l], jnp.zeros((D, LANES - E - N_EXPERT_GROUPS), f32)], axis=1)
    b_r = jnp.concatenate([b_expert_router[l], b_group_router[l], jnp.zeros((LANES - E - N_EXPERT_GROUPS,), f32)])[None, :]
    weg, weu, wed = (a[l].astype(bf16) for a in (w_exp_gate, w_exp_up, w_exp_down))
    g1, b1, g2, b2 = (a[l][None, :].astype(f32) for a in (ln1_g, ln1_b, ln2_g, ln2_b))
    ssm_w = (ssm_lambda_re[l], ssm_lambda_im[l], ssm_log_step[l], ssm_b_re[l], ssm_b_im[l], ssm_c_re[l], ssm_c_im[l], ssm_d[l])

    def tail(x2, attn, ssm_y, zn):
        n = x2.shape[0]
        x1, x1_rows = _merge(x2, attn, ssm_y, zn, c_ga // D, c_gb // D, wa, wg, ws, wo, g1, b1, _pick(n, (256, 128)))
        gates, route, counts = _router(x1, w_r, b_r, _pick(n, (512, 128)))
        if 2 * n >= E * MOE_TILE:
            return _moe_routed(x1, x1_rows, route, counts, weg, weu, wed, g2, b2, MOE_TILE, _pick(n, (256, 128)))
        return _moe(x1, gates, weg, weu, wed, g2, b2, _pick(n, (512, 128)))

    N = B * T
    xp2 = x_prompt.reshape(N, D)
    xpb = xp2.astype(bf16)
    tm = _pick(N, (1024, 512, 256, 128))
    zn = _mm_nn(xpb, w_n, tm, tn_n)
    zt = _mm_nt(w_t, xpb, 512, tm, bf16)
    wit = _mm_nt(w_wi, xpb, 2 * SUBLANES, tm, f32)
    k_p, v_p, ki_p = zn[:, c_k:c_k + d_attn], zn[:, c_v:c_v + d_attn], zn[:, c_s:c_s + IDX_DIM]
    tq = _pick(T, (256, 128))
    attn_p = _prompt_attention(zt[:d_attn], zt[d_attn:2 * d_attn], zt[2 * d_attn:], wit,
                               ki_p.reshape(B, T, IDX_DIM).astype(bf16), k_p.astype(bf16),
                               _prompt_bias_tiles(rel_bias, tq), B, T, tq)
    L = _pick(T, (16, 8, 4, 2))
    zeros_state = jnp.zeros((B, G, STATE_DIM), f32)
    y_p, h_p = _ssm(zn, c_u // LANES, d_ssm, B, _pack_state(zeros_state, zeros_state)[:, :, None, :],
                    _ssm_tables(*ssm_w, L), L, True)
    hr_p, hi_p = _unpack_state(h_p[:, :, 0, :], G, STATE_DIM)
    out_p = tail(xp2, attn_p, y_p, zn).reshape(B, T, D)

    Ns = Bd * Tn
    xs2 = x_sample.reshape(Ns, D)
    xsb = xs2.astype(bf16)
    zs = _mm_nn(xsb, w_n, Ns, tn_n)
    w_s = jnp.concatenate([cols(o_q, d_attn), cols(o_qi, d_qi)], axis=1)
    w_s = jnp.pad(w_s, ((0, 0), (0, _round_up(d_attn + d_qi, tn_n) - d_attn - d_qi))).astype(bf16)
    zq = _mm_nn(xsb, w_s, Ns, tn_n)
    q_s = zq[:, :d_attn].reshape(Bd, Tn, N_HEADS, HEAD_DIM)
    qi_s = zq[:, d_attn:d_attn + d_qi].reshape(Bd, Tn, IDX_HEADS, IDX_DIM)
    k_s = zs[:, c_k:c_k + d_attn].reshape(Bd, Tn, N_HEADS, HEAD_DIM)
    v_s = zs[:, c_v:c_v + d_attn].reshape(Bd, Tn, N_HEADS, HEAD_DIM)
    ki_s = zs[:, c_s:c_s + IDX_DIM].reshape(Bd, Tn, IDX_DIM)
    wi_s = zs[:, c_s + IDX_DIM:c_s + IDX_DIM + IDX_HEADS].reshape(Bd, Tn, IDX_HEADS)
    n_pages = page_table.shape[1]
    pp = _pick(n_pages, (16, 8, 4, 2, 1))
    sc_past, sc_new = _sample_scores(page_table, qi_s, wi_s, ki_s, cache_kidx[l], pp)
    idx_list = _sample_select(sc_past.reshape(Ns, n_pages * ps), sc_new.reshape(Ns, LANES), Tn)
    attn_s = _sample_attention(page_table, q_s, k_s, v_s, idx_list, rel_bias, cache_k[l], cache_v[l])
    y_s, h_s = _ssm(zs, c_u // LANES, d_ssm, 1, _pack_state(state_ssm_re[l], state_ssm_im[l])[:, None],
                    _ssm_tables(*ssm_w, Tn), Tn, False)
    hr_s, hi_s = _unpack_state(h_s[:, 0], G, STATE_DIM)
    out_s = tail(xs2, attn_s.reshape(Ns, d_attn), y_s, zs).reshape(Bd, Tn, D)

    sdt = state_ssm_re.dtype
    return (out_p, out_s,
            k_p.reshape(1, B, T // ps, ps, N_HEADS, HEAD_DIM), v_p.reshape(1, B, T // ps, ps, N_HEADS, HEAD_DIM),
            ki_p.reshape(1, B, T // ps, ps, IDX_DIM), hr_p.astype(sdt)[None], hi_p.astype(sdt)[None],
            k_s[None], v_s[None], ki_s[None], hr_s.astype(sdt)[None], hi_s.astype(sdt)[None])
```

```python
import functools
import math

import numpy as np
import jax
import jax.numpy as jnp
from jax import lax
from jax.experimental import pallas as pl
from jax.experimental.pallas import tpu as pltpu

f32, bf16, i32 = jnp.float32, jnp.bfloat16, jnp.int32

N_HEADS = 8
HEAD_DIM = 128
IDX_HEADS = 8
IDX_DIM = 64
TOPK_MAX = 256
NUM_BUCKETS = 32
MAX_DISTANCE = 128
SSM_GROUP = 16
STATE_DIM = 64
N_EXPERT_GROUPS = 4
EXPERTS_PER_GROUP = 8
TOP_K_EXPERT = 2
DEPTH = 1
ALPHA = (2 * DEPTH) ** 0.25
LN_EPS = 1e-5
INDEXER_SCALE = (IDX_HEADS ** -0.5) * (IDX_DIM ** -0.5)

LANES = 128
SUBLANES = 8
VMEM_LIMIT = 56 * 1024 * 1024
MOE_TILE = 256
DMA_UNROLL = 8

INT_MIN = np.int32(-2 ** 31)
KEY_NEG_INF = np.int32(-2139095041)
NEG_BIG = -1e30


def _cparams(sem):
    return pltpu.CompilerParams(dimension_semantics=sem, vmem_limit_bytes=VMEM_LIMIT)


def _round_up(a, b):
    return (a + b - 1) // b * b


def _mm_nn_kernel(a_ref, b_ref, o_ref):
    o_ref[...] = jnp.dot(a_ref[...], b_ref[...], preferred_element_type=f32).astype(o_ref.dtype)


def _mm_nn(a, b, tm, tn, out_dtype=f32):
    M, K = a.shape
    N = b.shape[1]
    return pl.pallas_call(
        _mm_nn_kernel, grid=(M // tm, N // tn),
        in_specs=[pl.BlockSpec((tm, K), lambda i, j: (i, 0)), pl.BlockSpec((K, tn), lambda i, j: (0, j))],
        out_specs=pl.BlockSpec((tm, tn), lambda i, j: (i, j)),
        out_shape=jax.ShapeDtypeStruct((M, N), out_dtype),
        compiler_params=_cparams(("parallel", "arbitrary")), name="proj_nn")(a, b)


def _mm_nt_kernel(w_ref, x_ref, o_ref):
    o_ref[...] = lax.dot_general(w_ref[...], x_ref[...], (((1,), (1,)), ((), ())),
                                 preferred_element_type=f32).astype(o_ref.dtype)


def _mm_nt(w, x, tn, tm, out_dtype):
    n, K = w.shape
    M = x.shape[0]
    return pl.pallas_call(
        _mm_nt_kernel, grid=(M // tm, n // tn),
        in_specs=[pl.BlockSpec((tn, K), lambda i, j: (j, 0)), pl.BlockSpec((tm, K), lambda i, j: (i, 0))],
        out_specs=pl.BlockSpec((tn, tm), lambda i, j: (j, i)),
        out_shape=jax.ShapeDtypeStruct((n, M), out_dtype),
        compiler_params=_cparams(("parallel", "arbitrary")), name="proj_nt")(w, x)


def _order_key(x):
    b = lax.bitcast_convert_type(x + 0.0, i32)
    return b ^ ((b >> 31) & np.int32(0x7FFFFFFF))


def _bucket_starts():
    me = NUM_BUCKETS // 2
    d = np.arange(1, 4 * MAX_DISTANCE)
    large = me + (np.log(d.astype(np.float32) / np.float32(me)) / np.float32(math.log(MAX_DISTANCE / me))
                  * np.float32(NUM_BUCKETS - me)).astype(np.int32)
    bucket = np.where(d < me, d, np.minimum(large, NUM_BUCKETS - 1))
    return [int(d[np.argmax(bucket >= b)]) for b in range(me + 1, NUM_BUCKETS)]


def _rel_bucket(dist):
    me = NUM_BUCKETS // 2
    d = jnp.maximum(dist, 0)
    large = me
    for start in _bucket_starts():
        large = large + jnp.where(d >= start, 1, 0)
    return jnp.where(d < me, d, large)


def _select_rows(keys_sc, n_rows, k_top, idx_bits, slab):
    tq = keys_sc.shape[1]
    n_slabs = n_rows // slab

    def count(pred):
        def body(i, cnt):
            r0 = pl.multiple_of(i * slab, slab)
            blk = keys_sc[pl.ds(r0, slab), :]
            hit = jnp.where(pred(blk, r0), 1, 0).astype(i32)
            return cnt + jnp.sum(hit.reshape(slab // SUBLANES, SUBLANES, tq), axis=0)
        cnt = lax.fori_loop(0, n_slabs, body, jnp.zeros((SUBLANES, tq), i32))
        return jnp.sum(cnt, axis=0, keepdims=True)

    def count_ge(cand):
        cb = jnp.broadcast_to(cand, (slab, tq))
        return count(lambda blk, r0: blk >= cb)

    zero = jnp.zeros((1, tq), i32)
    thr = jnp.where(count_ge(zero) >= k_top, zero, jnp.full((1, tq), INT_MIN, i32))

    def bit_body(it, thr):
        cand = thr | lax.shift_left(np.int32(1), np.int32(30) - it)
        return jnp.where(count_ge(cand) >= k_top, cand, thr)

    thr = lax.fori_loop(0, 31, bit_body, thr)
    thr = jnp.maximum(thr, KEY_NEG_INF)
    n_ge = count_ge(thr)

    @pl.when(jnp.max(n_ge) > k_top)
    def _():
        need = k_top - count_ge(thr + 1)
        tb = jnp.broadcast_to(thr, (slab, tq))

        def count_eq_below(m):
            mb = jnp.broadcast_to(m, (slab, tq))
            return count(lambda blk, r0: (blk == tb) & (r0 + lax.broadcasted_iota(i32, (slab, tq), 0) < mb))

        def idx_body(it, m):
            cand = m | lax.shift_left(np.int32(1), np.int32(idx_bits - 1) - it)
            return jnp.where(count_eq_below(cand) < need, cand, m)

        m = lax.fori_loop(0, idx_bits, idx_body, zero)
        mb = jnp.broadcast_to(m, (slab, tq))

        def demote(i, _):
            r0 = pl.multiple_of(i * slab, slab)
            blk = keys_sc[pl.ds(r0, slab), :]
            drop = (blk == tb) & (r0 + lax.broadcasted_iota(i32, (slab, tq), 0) > mb)
            keys_sc[pl.ds(r0, slab), :] = jnp.where(drop, blk - 1, blk)
            return 0

        lax.fori_loop(0, n_slabs, demote, 0)

    return thr


def _pattn_kernel(qiT_ref, wiT_ref, ki_ref, qT_ref, k_ref, vT_ref, bias_ref, o_ref,
                  keys_sc, thr_sc, msk_sc, m_sc, l_sc, acc_sc, *, k_top, tq, idx_bits):
    qb, kb = _folded_blocks(pl.program_id(1), pl.program_id(2), pl.num_programs(2) - 1)

    @pl.when(kb == 0)
    def _():
        def chunk(c, _):
            r0 = pl.multiple_of(c * tq, tq)
            ki_c = ki_ref[0, pl.ds(r0, tq), :]
            score = jnp.zeros((tq, tq), f32)
            for h in range(IDX_HEADS):
                s = jnp.dot(ki_c, qiT_ref[h * IDX_DIM:(h + 1) * IDX_DIM, :], preferred_element_type=f32)
                score = score + wiT_ref[h:h + 1, :] * jnp.maximum(s, 0.0)
            kpos = r0 + lax.broadcasted_iota(i32, (tq, tq), 0)
            qpos = qb * tq + lax.broadcasted_iota(i32, (tq, tq), 1)
            keys_sc[pl.ds(r0, tq), :] = jnp.where(kpos <= qpos, _order_key(score), INT_MIN)
            return 0

        lax.fori_loop(0, qb + 1, chunk, 0)
        thr_sc[...] = _select_rows(keys_sc, (qb + 1) * tq, k_top, idx_bits, tq)
        m_sc[...] = jnp.full(m_sc.shape, NEG_BIG, f32)
        l_sc[...] = jnp.zeros(l_sc.shape, f32)
        acc_sc[...] = jnp.zeros(acc_sc.shape, f32)

    r0 = pl.multiple_of(kb * tq, tq)
    msk_sc[...] = jnp.where(keys_sc[pl.ds(r0, tq), :] >= thr_sc[...], 0.0, NEG_BIG)
    for h in range(N_HEADS):
        hs = slice(h * HEAD_DIM, (h + 1) * HEAD_DIM)
        for c in range(tq // LANES):
            cs = slice(c * LANES, (c + 1) * LANES)
            s = (jnp.dot(k_ref[:, hs], qT_ref[hs, cs], preferred_element_type=f32)
                 + bias_ref[h, 0, :, cs] + msk_sc[:, cs])
            m_old = m_sc[h:h + 1, cs]
            m_new = jnp.maximum(m_old, jnp.max(s, axis=0, keepdims=True))
            p = jnp.exp(s - m_new)
            alpha = jnp.exp(m_old - m_new)
            l_sc[h:h + 1, cs] = alpha * l_sc[h:h + 1, cs] + jnp.sum(p, axis=0, keepdims=True)
            acc_sc[hs, cs] = alpha * acc_sc[hs, cs] + jnp.dot(vT_ref[hs, :], p.astype(bf16),
                                                              preferred_element_type=f32)
            m_sc[h:h + 1, cs] = m_new

    @pl.when(kb == qb)
    def _():
        for h in range(N_HEADS):
            hs = slice(h * HEAD_DIM, (h + 1) * HEAD_DIM)
            o_ref[:, hs] = jnp.transpose(acc_sc[hs, :] / l_sc[h:h + 1, :])


def _folded_blocks(i, j, nq):
    first = j <= i
    return jnp.where(first, i, nq - 1 - i), jnp.where(first, j, j - i - 1)


def _prompt_attention(qT, vT, qiT, wiT, ki, k, bias_tiles, B, T, tq):
    nq = T // tq
    assert nq % 2 == 0
    k_top = min(TOPK_MAX, T // 4)
    d_attn = N_HEADS * HEAD_DIM
    kern = functools.partial(_pattn_kernel, k_top=k_top, tq=tq, idx_bits=max(1, (T - 1).bit_length()))
    qblk = lambda i, j: _folded_blocks(i, j, nq)[0]
    kblk = lambda i, j: _folded_blocks(i, j, nq)[1]
    col = lambda b, i, j: (0, b * nq + qblk(i, j))
    return pl.pallas_call(
        kern, grid=(B, nq // 2, nq + 1),
        in_specs=[
            pl.BlockSpec((IDX_HEADS * IDX_DIM, tq), col),
            pl.BlockSpec((IDX_HEADS, tq), col),
            pl.BlockSpec((1, T, IDX_DIM), lambda b, i, j: (b, 0, 0)),
            pl.BlockSpec((d_attn, tq), col),
            pl.BlockSpec((tq, d_attn), lambda b, i, j: (b * nq + kblk(i, j), 0)),
            pl.BlockSpec((d_attn, tq), lambda b, i, j: (0, b * nq + kblk(i, j))),
            pl.BlockSpec((N_HEADS, 1, tq, tq), lambda b, i, j: (0, jnp.minimum(qblk(i, j) - kblk(i, j), 2), 0, 0)),
        ],
        out_specs=pl.BlockSpec((tq, d_attn), lambda b, i, j: (b * nq + qblk(i, j), 0)),
        out_shape=jax.ShapeDtypeStruct((B * T, d_attn), f32),
        scratch_shapes=[pltpu.VMEM((T, tq), i32), pltpu.VMEM((1, tq), i32), pltpu.VMEM((tq, tq), f32),
                        pltpu.VMEM((N_HEADS, tq), f32), pltpu.VMEM((N_HEADS, tq), f32),
                        pltpu.VMEM((d_attn, tq), f32)],
        compiler_params=_cparams(("parallel", "arbitrary", "arbitrary")), name="prompt_attn",
    )(qiT, wiT, ki, qT, k, vT, bias_tiles)


def _bias_lookup_kernel(rb_ref, bkt_ref, o_ref):
    bkt = bkt_ref[0]
    for h in range(N_HEADS):
        acc = jnp.zeros(bkt.shape, f32)
        for b in range(NUM_BUCKETS):
            acc = jnp.where(bkt == b, rb_ref[b, h], acc)
        o_ref[h, 0] = acc


def _prompt_bias_tiles(rel_bias, tq):
    s = jnp.arange(tq, dtype=i32)[:, None]
    t = jnp.arange(tq, dtype=i32)[None, :]
    bkt = jnp.stack([_rel_bucket(d * tq + t - s) for d in range(3)]).astype(i32)
    return pl.pallas_call(
        _bias_lookup_kernel, grid=(3,),
        in_specs=[pl.BlockSpec(memory_space=pltpu.SMEM), pl.BlockSpec((1, tq, tq), lambda d: (d, 0, 0))],
        out_specs=pl.BlockSpec((N_HEADS, 1, tq, tq), lambda d: (0, d, 0, 0)),
        out_shape=jax.ShapeDtypeStruct((N_HEADS, 3, tq, tq), f32),
        compiler_params=_cparams(("parallel",)), name="bias_tiles")(rel_bias.astype(f32), bkt)


def _ssm_tables(lam_re, lam_im, log_step, b_re, b_im, c_re, c_im, d, L):
    hp = lax.Precision.HIGHEST
    G, P = lam_re.shape
    H = b_re.shape[-1]
    lr, li = lam_re.astype(f32), lam_im.astype(f32)
    step = jnp.exp(log_step.astype(f32))[:, None]
    taus = jnp.arange(L + 1, dtype=f32)[:, None, None]
    mag = jnp.exp(lr * step * taus)
    pr, pi = mag * jnp.cos(li * step * taus), mag * jnp.sin(li * step * taus)
    x, y, den = pr[1] - 1.0, pi[1], lr * lr + li * li
    fr, fi = (x * lr + y * li) / den, (y * lr - x * li) / den
    bbr = fr[..., None] * b_re - fi[..., None] * b_im
    bbi = fr[..., None] * b_im + fi[..., None] * b_re
    clr = c_re[None] * pr[:, :, None, :] - c_im[None] * pi[:, :, None, :]
    cli = c_re[None] * pi[:, :, None, :] + c_im[None] * pr[:, :, None, :]
    kern = (jnp.einsum('tghp,gpk->tghk', clr[:L], bbr, precision=hp)
            - jnp.einsum('tghp,gpk->tghk', cli[:L], bbi, precision=hp))
    kern = kern.at[0].add(jnp.eye(H, dtype=f32)[None] * d.astype(f32)[:, :, None])
    GB = LANES // H
    J = G // GB
    rev = jnp.arange(L - 1, -1, -1)
    wr = pr[rev][:, :, :, None] * bbr[None] - pi[rev][:, :, :, None] * bbi[None]
    wi = pr[rev][:, :, :, None] * bbi[None] + pi[rev][:, :, :, None] * bbr[None]
    ka = jnp.transpose(kern.reshape(L, J, GB, H, H), (1, 0, 2, 4, 3)).reshape(J, L, LANES, H)
    to_wa = lambda a: jnp.transpose(a.reshape(L, J, GB, P, H), (1, 0, 2, 4, 3)).reshape(J, L, LANES, P)
    wa = jnp.stack([to_wa(wr), to_wa(wi)], axis=1)
    to_va = lambda a: jnp.transpose(a.reshape(L, J, GB, H, P), (1, 0, 2, 4, 3)).reshape(J, L, GB * P, H)
    va = jnp.stack([to_va(clr[1:]), -to_va(cli[1:])], axis=1)
    mT, w, v = pl.pallas_call(
        functools.partial(_ssm_expand_kernel, L=L), grid=(J,),
        in_specs=[pl.BlockSpec((1, L, LANES, H), lambda j: (j, 0, 0, 0)),
                  pl.BlockSpec((1, 2, L, LANES, P), lambda j: (j, 0, 0, 0, 0)),
                  pl.BlockSpec((1, 2, L, GB * P, H), lambda j: (j, 0, 0, 0, 0))],
        out_specs=[pl.BlockSpec((1, L * LANES, L * LANES), lambda j: (j, 0, 0)),
                   pl.BlockSpec((1, L * LANES, 2 * GB * P), lambda j: (j, 0, 0)),
                   pl.BlockSpec((1, 2 * GB * P, L * LANES), lambda j: (j, 0, 0))],
        out_shape=[jax.ShapeDtypeStruct((J, L * LANES, L * LANES), bf16),
                   jax.ShapeDtypeStruct((J, L * LANES, 2 * GB * P), bf16),
                   jax.ShapeDtypeStruct((J, 2 * GB * P, L * LANES), bf16)],
        compiler_params=_cparams(("parallel",)), name="ssm_tables")(ka, wa, va)
    lam = jnp.concatenate([pr[L].reshape(J, 1, GB * P), pi[L].reshape(J, 1, GB * P)], axis=2)
    return mT, w, v, lam


def _ssm_expand_kernel(ka_ref, wa_ref, va_ref, mt_ref, w_ref, v_ref, *, L):
    H = ka_ref.shape[3]
    P = wa_ref.shape[4]
    S = v_ref.shape[1] // 2

    def repeat(n, width):
        return jnp.where(lax.broadcasted_iota(i32, (n, width), 1) % n == lax.broadcasted_iota(i32, (n, width), 0),
                         1.0, 0.0).astype(bf16)

    def same_group(rows, rdiv, cols, cdiv):
        return (lax.broadcasted_iota(i32, (rows, cols), 0) // rdiv) == (lax.broadcasted_iota(i32, (rows, cols), 1) // cdiv)

    rep_h, rep_p = repeat(H, LANES), repeat(P, S)
    m_mask, w_mask, v_mask = same_group(LANES, H, LANES, H), same_group(LANES, H, S, P), same_group(S, P, LANES, H)
    zero = jnp.zeros((LANES, LANES), bf16)
    for tau in range(L):
        tile = jnp.where(m_mask, jnp.dot(ka_ref[0, tau].astype(bf16), rep_h, preferred_element_type=f32), 0.0).astype(bf16)
        for s in range(L - tau):
            mt_ref[0, s * LANES:(s + 1) * LANES, (s + tau) * LANES:(s + tau + 1) * LANES] = tile
    for s in range(L):
        for t in range(s):
            mt_ref[0, s * LANES:(s + 1) * LANES, t * LANES:(t + 1) * LANES] = zero
    for half in range(2):
        for s in range(L):
            wt = jnp.dot(wa_ref[0, half, s].astype(bf16), rep_p, preferred_element_type=f32)
            w_ref[0, s * LANES:(s + 1) * LANES, half * S:(half + 1) * S] = jnp.where(w_mask, wt, 0.0).astype(bf16)
            vt = jnp.dot(va_ref[0, half, s].astype(bf16), rep_h, preferred_element_type=f32)
            v_ref[0, half * S:(half + 1) * S, s * LANES:(s + 1) * LANES] = jnp.where(v_mask, vt, 0.0).astype(bf16)


def _ssm_kernel(u_ref, mt_ref, w_ref, v_ref, lam_ref, h0_ref, y_ref, hout_ref, uc_sc, s_sc, hin_sc, yc_sc,
                *, L, rows, seq):
    S = s_sc.shape[1] // 2
    for s in range(L):
        uc_sc[:, s * LANES:(s + 1) * LANES] = u_ref[pl.ds(s, rows, stride=L), :].astype(bf16)
    uc = uc_sc[...]
    s_sc[...] = jnp.dot(uc, w_ref[0], preferred_element_type=f32)
    lr, li = lam_ref[0, :, 0:S], lam_ref[0, :, S:2 * S]
    if seq:
        def body(c, carry):
            hr, hi = carry
            hin_sc[pl.ds(c, 1), 0:S] = hr
            hin_sc[pl.ds(c, 1), S:2 * S] = hi
            sr, si = s_sc[pl.ds(c, 1), 0:S], s_sc[pl.ds(c, 1), S:2 * S]
            return lr * hr - li * hi + sr, lr * hi + li * hr + si

        hr, hi = lax.fori_loop(0, rows, body, (h0_ref[0, 0, :, 0:S], h0_ref[0, 0, :, S:2 * S]))
    else:
        hr0, hi0 = h0_ref[0, 0, :, 0:S], h0_ref[0, 0, :, S:2 * S]
        hin_sc[...] = h0_ref[0, 0]
        hr = lr * hr0 - li * hi0 + s_sc[:, 0:S]
        hi = lr * hi0 + li * hr0 + s_sc[:, S:2 * S]
    hout_ref[0, 0, :, 0:S] = hr
    hout_ref[0, 0, :, S:2 * S] = hi
    yc_sc[...] = (jnp.dot(uc, mt_ref[0], preferred_element_type=f32)
                  + jnp.dot(hin_sc[...].astype(bf16), v_ref[0], preferred_element_type=f32))
    for t in range(L):
        y_ref[pl.ds(t, rows, stride=L), :] = yc_sc[:, t * LANES:(t + 1) * LANES]


def _ssm(zn, u_blk0, d_ssm, n_steps, h0, tables, L, seq):
    mT, w, v, lam = tables
    N = zn.shape[0]
    J = d_ssm // LANES
    tok = N // n_steps
    rows = tok // L
    rows_h = h0.shape[2]
    S2 = h0.shape[3]
    kern = functools.partial(_ssm_kernel, L=L, rows=rows, seq=seq)
    return pl.pallas_call(
        kern, grid=(J, n_steps),
        in_specs=[
            pl.BlockSpec((tok, LANES), lambda j, b: (b, u_blk0 + j)),
            pl.BlockSpec((1, L * LANES, L * LANES), lambda j, b: (j, 0, 0)),
            pl.BlockSpec((1, L * LANES, S2), lambda j, b: (j, 0, 0)),
            pl.BlockSpec((1, S2, L * LANES), lambda j, b: (j, 0, 0)),
            pl.BlockSpec((1, 1, S2), lambda j, b: (j, 0, 0)),
            pl.BlockSpec((1, 1, rows_h, S2), lambda j, b: (j, b, 0, 0)),
        ],
        out_specs=[pl.BlockSpec((tok, LANES), lambda j, b: (b, j)),
                   pl.BlockSpec((1, 1, rows_h, S2), lambda j, b: (j, b, 0, 0))],
        out_shape=[jax.ShapeDtypeStruct((N, d_ssm), f32),
                   jax.ShapeDtypeStruct(h0.shape, f32)],
        scratch_shapes=[pltpu.VMEM((rows, L * LANES), bf16), pltpu.VMEM((rows, S2), f32),
                        pltpu.VMEM((rows, S2), f32), pltpu.VMEM((rows, L * LANES), f32)],
        compiler_params=_cparams(("parallel", "arbitrary")), name="ssm",
    )(zn, mT, w, v, lam, h0)


def _pack_state(a_re, a_im):
    Bn, G, P = a_re.shape
    J = G * SSM_GROUP // LANES
    blk = lambda a: jnp.transpose(a.astype(f32).reshape(Bn, J, (G // J) * P), (1, 0, 2))
    return jnp.concatenate([blk(a_re), blk(a_im)], axis=2)


def _unpack_state(h, G, P):
    J, Bn, S2 = h.shape
    un = lambda a: jnp.transpose(a, (1, 0, 2)).reshape(Bn, G, P)
    return un(h[:, :, :S2 // 2]), un(h[:, :, S2 // 2:])


def _sigmoid(x):
    return 1.0 / (1.0 + jnp.exp(-x))


def _gelu_tanh(x):
    return 0.5 * x * (1.0 + jnp.tanh(math.sqrt(2.0 / math.pi) * (x + 0.044715 * (x * x * x))))


def _layer_norm(r, g, b):
    mu = jnp.mean(r, axis=-1, keepdims=True)
    c = r - mu
    var = jnp.mean(c * c, axis=-1, keepdims=True)
    return c * lax.rsqrt(var + LN_EPS) * g + b


def _merge_kernel(x_ref, attn_ref, y_ref, ga_ref, gb_ref, wa_ref, wg_ref, ws_ref, wo_ref, g_ref, b_ref, o_ref, or_ref):
    a = jnp.dot(attn_ref[...].astype(bf16), wa_ref[...], preferred_element_type=f32)
    z = _gelu_tanh(y_ref[...])
    gl = jnp.dot(z.astype(bf16), wg_ref[...], preferred_element_type=f32)
    s = jnp.dot((z * _sigmoid(gl)).astype(bf16), ws_ref[...], preferred_element_type=f32)
    m = _sigmoid(ga_ref[...]) * a + _sigmoid(gb_ref[...]) * s
    mix = jnp.dot(m.astype(bf16), wo_ref[...], preferred_element_type=f32)
    o_ref[...] = _layer_norm(ALPHA * x_ref[...] + mix, g_ref[...], b_ref[...])
    _store_token_rows(or_ref, o_ref)


def _store_token_rows(rows_ref, src_ref):
    n, D = src_ref.shape
    for c in range(D // LANES):
        rows_ref[pl.ds(c, n, stride=D // LANES), :] = src_ref[:, c * LANES:(c + 1) * LANES]


def _load_token_rows(dst_ref, rows_ref, n):
    D = dst_ref.shape[1]
    for c in range(D // LANES):
        dst_ref[:, c * LANES:(c + 1) * LANES] = rows_ref[pl.ds(c, n, stride=D // LANES), :].astype(dst_ref.dtype)


def _merge(x, attn, ssm_y, zn, ga_blk, gb_blk, wa, wg, ws, wo, g, b, tm):
    N, D = x.shape
    Dh = attn.shape[1]
    const = lambda shape: pl.BlockSpec(shape, lambda i: (0, 0), pipeline_mode=pl.Buffered(1))
    return pl.pallas_call(
        _merge_kernel, grid=(N // tm,),
        in_specs=[pl.BlockSpec((tm, D), lambda i: (i, 0)),
                  pl.BlockSpec((tm, Dh), lambda i: (i, 0)),
                  pl.BlockSpec((tm, Dh), lambda i: (i, 0)),
                  pl.BlockSpec((tm, D), lambda i: (i, ga_blk)),
                  pl.BlockSpec((tm, D), lambda i: (i, gb_blk)),
                  const((Dh, D)), const((Dh, Dh)), const((Dh, D)), const((D, D)),
                  const((1, D)), const((1, D))],
        out_specs=[pl.BlockSpec((tm, D), lambda i: (i, 0)),
                   pl.BlockSpec((tm * (D // LANES), LANES), lambda i: (i, 0))],
        out_shape=[jax.ShapeDtypeStruct((N, D), f32), jax.ShapeDtypeStruct((N * (D // LANES), LANES), f32)],
        compiler_params=_cparams(("parallel",)), name="merge_ln1",
    )(x, attn, ssm_y, zn, zn, wa, wg, ws, wo, g, b)


ROUTE_E1, ROUTE_E2, ROUTE_W1, ROUTE_W2, ROUTE_R1, ROUTE_R2 = range(6)


def _router_kernel(x_ref, w_ref, b_ref, o_ref, route_ref, cnt_ref, run_sc):
    E = N_EXPERT_GROUPS * EXPERTS_PER_GROUP

    @pl.when(pl.program_id(0) == 0)
    def _():
        run_sc[...] = jnp.zeros(run_sc.shape, f32)

    logit = jnp.dot(x_ref[...], w_ref[...], preferred_element_type=f32, precision=lax.Precision.HIGHEST) + b_ref[...]
    lane = lax.broadcasted_iota(i32, logit.shape, 1)
    is_g = (lane >= E) & (lane < E + N_EXPERT_GROUPS)
    glog = jnp.where(is_g, logit, -jnp.inf)
    g_max = jnp.max(glog, axis=1, keepdims=True)
    g_idx = jnp.min(jnp.where(glog == g_max, lane, 4 * LANES), axis=1, keepdims=True) - E
    g_w = 1.0 / jnp.sum(jnp.where(is_g, jnp.exp(glog - g_max), 0.0), axis=1, keepdims=True)
    in_g = (lane < E) & (lane // EXPERTS_PER_GROUP == g_idx)
    e1 = jnp.where(in_g, logit, -jnp.inf)
    v1 = jnp.max(e1, axis=1, keepdims=True)
    i1 = jnp.min(jnp.where(e1 == v1, lane, 4 * LANES), axis=1, keepdims=True)
    e2 = jnp.where(lane == i1, -jnp.inf, e1)
    v2 = jnp.max(e2, axis=1, keepdims=True)
    i2 = jnp.min(jnp.where(e2 == v2, lane, 4 * LANES), axis=1, keepdims=True)
    t = jnp.exp(v2 - v1)
    w1 = g_w / (1.0 + t)
    w2 = g_w * t / (1.0 + t)
    o_ref[...] = jnp.where(lane == i1, w1, 0.0) + jnp.where(lane == i2, w2, 0.0)
    tm = logit.shape[0]
    hit = jnp.where((lane == i1) | (lane == i2), 1.0, 0.0)
    before = jnp.where(lax.broadcasted_iota(i32, (tm, tm), 0) > lax.broadcasted_iota(i32, (tm, tm), 1), 1.0, 0.0)
    rank = jnp.dot(before.astype(bf16), hit.astype(bf16), preferred_element_type=f32) + run_sc[...]
    r1 = jnp.sum(jnp.where(lane == i1, rank, 0.0), axis=1, keepdims=True)
    r2 = jnp.sum(jnp.where(lane == i2, rank, 0.0), axis=1, keepdims=True)
    rec = jnp.zeros(logit.shape, f32)
    for pos, val in ((ROUTE_E1, i1.astype(f32)), (ROUTE_E2, i2.astype(f32)), (ROUTE_W1, w1), (ROUTE_W2, w2),
                     (ROUTE_R1, r1), (ROUTE_R2, r2)):
        rec = jnp.where(lane == pos, val, rec)
    route_ref[...] = rec
    run_sc[...] += jnp.sum(hit, axis=0, keepdims=True)
    cnt_ref[...] = run_sc[...]


def _router(x1, w_r, b_r, tm):
    N, D = x1.shape
    tile = pl.BlockSpec((tm, LANES), lambda i: (i, 0))
    return pl.pallas_call(
        _router_kernel, grid=(N // tm,),
        in_specs=[pl.BlockSpec((tm, D), lambda i: (i, 0)),
                  pl.BlockSpec((D, LANES), lambda i: (0, 0)),
                  pl.BlockSpec((1, LANES), lambda i: (0, 0))],
        out_specs=[tile, tile, pl.BlockSpec((1, LANES), lambda i: (0, 0))],
        out_shape=[jax.ShapeDtypeStruct((N, LANES), f32), jax.ShapeDtypeStruct((N, LANES), f32),
                   jax.ShapeDtypeStruct((1, LANES), f32)],
        scratch_shapes=[pltpu.VMEM((1, LANES), f32)],
        compiler_params=_cparams(("arbitrary",)), name="router",
    )(x1, w_r, b_r)


def _moe_routed_kernel(te_ref, nu_ref, inv_ref, x_hbm, wg_ref, wu_ref, wd_ref, o_ref, xbuf, xb_sc, y_sc, sem, *, R, C):
    t = pl.program_id(0)
    n_used = nu_ref[0]
    slot = t % 2

    def row_copy(tt, sl, j):
        tok = inv_ref[tt * R + j]
        return pltpu.make_async_copy(x_hbm.at[pl.ds(tok * C, C), :], xbuf.at[sl, pl.ds(j * C, C), :], sem.at[sl])

    def issue(tt, sl):
        def body(j, _):
            row_copy(tt, sl, j).start()
            return 0
        lax.fori_loop(0, R, body, 0, unroll=DMA_UNROLL)

    @pl.when(t == 0)
    def _():
        issue(0, 0)

    @pl.when(t + 1 < n_used)
    def _():
        issue(t + 1, 1 - slot)

    @pl.when(t < n_used)
    def _():
        def wait_body(j, _):
            row_copy(t, slot, j).wait()
            return 0
        lax.fori_loop(0, R, wait_body, 0, unroll=DMA_UNROLL)
        _load_token_rows(xb_sc, xbuf.at[slot], R)
        xb = xb_sc[...]
        hg = jnp.dot(xb, wg_ref[0], preferred_element_type=f32)
        hu = jnp.dot(xb, wu_ref[0], preferred_element_type=f32)
        h = hg * _sigmoid(hg) * hu
        y_sc[...] = jnp.dot(h.astype(bf16), wd_ref[0], preferred_element_type=f32)
        _store_token_rows(o_ref, y_sc)

    @pl.when(t >= n_used)
    def _():
        o_ref[...] = jnp.zeros(o_ref.shape, f32)


def _moe_combine_kernel(p1_ref, p2_ref, x_ref, route_ref, g_ref, b_ref, y_hbm, o_ref, ybuf, r_sc, sem, *, tm, C):
    i = pl.program_id(0)
    n = pl.num_programs(0)
    slot = i % 2

    def row_copies(ii, sl, j):
        a = pltpu.make_async_copy(y_hbm.at[pl.ds(p1_ref[ii * tm + j] * C, C), :], ybuf.at[sl, 0, pl.ds(j * C, C), :], sem.at[sl])
        b = pltpu.make_async_copy(y_hbm.at[pl.ds(p2_ref[ii * tm + j] * C, C), :], ybuf.at[sl, 1, pl.ds(j * C, C), :], sem.at[sl])
        return a, b

    def issue(ii, sl):
        def body(j, _):
            a, b = row_copies(ii, sl, j)
            a.start(priority=0)
            b.start(priority=1)
            return 0
        lax.fori_loop(0, tm, body, 0, unroll=DMA_UNROLL)

    @pl.when(i == 0)
    def _():
        issue(0, 0)

    @pl.when(i + 1 < n)
    def _():
        issue(i + 1, 1 - slot)

    def wait_body(j, _):
        a, b = row_copies(i, slot, j)
        a.wait()
        b.wait()
        return 0

    lax.fori_loop(0, tm, wait_body, 0, unroll=DMA_UNROLL)
    route = route_ref[...]
    lane = lax.broadcasted_iota(i32, route.shape, 1)
    w1 = jnp.sum(jnp.where(lane == ROUTE_W1, route, 0.0), axis=1, keepdims=True)
    w2 = jnp.sum(jnp.where(lane == ROUTE_W2, route, 0.0), axis=1, keepdims=True)
    for c in range(C):
        cs = slice(c * LANES, (c + 1) * LANES)
        y1 = ybuf[slot, 0, pl.ds(c, tm, stride=C), :]
        y2 = ybuf[slot, 1, pl.ds(c, tm, stride=C), :]
        r_sc[:, cs] = ALPHA * x_ref[:, cs] + (w1 * y1 + w2 * y2)
    o_ref[...] = _layer_norm(r_sc[...], g_ref[...], b_ref[...])


def _moe_routed(x1, x1_rows, route, counts, wg, wu, wd, g, b, R, tm):
    N, D = x1.shape
    C = D // LANES
    E, _, De = wg.shape
    n_tiles = (2 * N + E * (R - 1)) // R + 1
    col = lambda k: route[:, k]
    e1, e2 = col(ROUTE_E1).astype(i32), col(ROUTE_E2).astype(i32)
    cnt = counts[0, :E].astype(i32)
    padded = (cnt + R - 1) // R * R
    seg_end = jnp.cumsum(padded)
    seg_start = seg_end - padded
    pos1 = seg_start[e1] + col(ROUTE_R1).astype(i32)
    pos2 = seg_start[e2] + col(ROUTE_R2).astype(i32)
    tile_start = jnp.arange(n_tiles, dtype=i32) * R
    tile_expert = jnp.minimum(jnp.sum((tile_start[:, None] >= seg_end[None, :]).astype(i32), axis=1), E - 1)
    n_used = (seg_end[E - 1] // R).reshape(1)
    tok = jnp.arange(N, dtype=i32)
    inv = jnp.zeros((n_tiles * R,), i32).at[pos1].set(tok).at[pos2].set(tok)
    wspec = lambda shape: pl.BlockSpec(shape, lambda t, te, nu, iv: (te[t], 0, 0))
    ys = pl.pallas_call(
        functools.partial(_moe_routed_kernel, R=R, C=C),
        grid_spec=pltpu.PrefetchScalarGridSpec(
            num_scalar_prefetch=3, grid=(n_tiles,),
            in_specs=[pl.BlockSpec(memory_space=pl.ANY), wspec((1, D, De)), wspec((1, D, De)), wspec((1, De, D))],
            out_specs=pl.BlockSpec((R * C, LANES), lambda t, te, nu, iv: (t, 0)),
            scratch_shapes=[pltpu.VMEM((2, R * C, LANES), f32), pltpu.VMEM((R, D), bf16), pltpu.VMEM((R, D), f32),
                            pltpu.SemaphoreType.DMA((2,))]),
        out_shape=jax.ShapeDtypeStruct((n_tiles * R * C, LANES), f32),
        compiler_params=pltpu.CompilerParams(dimension_semantics=("arbitrary",), vmem_limit_bytes=VMEM_LIMIT,
                                             disable_bounds_checks=True),
        name="moe_experts",
    )(tile_expert, n_used, inv, x1_rows, wg, wu, wd)
    return pl.pallas_call(
        functools.partial(_moe_combine_kernel, tm=tm, C=C),
        grid_spec=pltpu.PrefetchScalarGridSpec(
            num_scalar_prefetch=2, grid=(N // tm,),
            in_specs=[pl.BlockSpec((tm, D), lambda i, p1, p2: (i, 0)),
                      pl.BlockSpec((tm, LANES), lambda i, p1, p2: (i, 0)),
                      pl.BlockSpec((1, D), lambda i, p1, p2: (0, 0)),
                      pl.BlockSpec((1, D), lambda i, p1, p2: (0, 0)),
                      pl.BlockSpec(memory_space=pl.ANY)],
            out_specs=pl.BlockSpec((tm, D), lambda i, p1, p2: (i, 0)),
            scratch_shapes=[pltpu.VMEM((2, 2, tm * C, LANES), f32), pltpu.VMEM((tm, D), f32),
                            pltpu.SemaphoreType.DMA((2,))]),
        out_shape=jax.ShapeDtypeStruct((N, D), f32),
        compiler_params=pltpu.CompilerParams(dimension_semantics=("arbitrary",), vmem_limit_bytes=VMEM_LIMIT,
                                             disable_bounds_checks=True),
        name="moe_combine_ln2",
    )(pos1, pos2, x1, route, g, b, ys)


def _moe_kernel(x_ref, gate_ref, wg_ref, wu_ref, wd_ref, g_ref, b_ref, o_ref, xb_sc, acc_sc):
    e = pl.program_id(1)

    @pl.when(e == 0)
    def _():
        xb_sc[...] = x_ref[...].astype(bf16)
        acc_sc[...] = jnp.zeros(acc_sc.shape, f32)

    gates = gate_ref[...]
    lane = lax.broadcasted_iota(i32, gates.shape, 1)
    ge = jnp.sum(jnp.where(lane == e, gates, 0.0), axis=1, keepdims=True)
    xb = xb_sc[...]
    hg = jnp.dot(xb, wg_ref[0], preferred_element_type=f32)
    hu = jnp.dot(xb, wu_ref[0], preferred_element_type=f32)
    h = hg * _sigmoid(hg) * hu
    acc_sc[...] += jnp.dot((h * ge).astype(bf16), wd_ref[0], preferred_element_type=f32)

    @pl.when(e == pl.num_programs(1) - 1)
    def _():
        o_ref[...] = _layer_norm(ALPHA * x_ref[...] + acc_sc[...], g_ref[...], b_ref[...])


def _moe(x1, gates, wg, wu, wd, g, b, tm):
    N, D = x1.shape
    E, _, De = wg.shape
    return pl.pallas_call(
        _moe_kernel, grid=(N // tm, E),
        in_specs=[pl.BlockSpec((tm, D), lambda i, e: (i, 0)),
                  pl.BlockSpec((tm, LANES), lambda i, e: (i, 0)),
                  pl.BlockSpec((1, D, De), lambda i, e: (e, 0, 0)),
                  pl.BlockSpec((1, D, De), lambda i, e: (e, 0, 0)),
                  pl.BlockSpec((1, De, D), lambda i, e: (e, 0, 0)),
                  pl.BlockSpec((1, D), lambda i, e: (0, 0)),
                  pl.BlockSpec((1, D), lambda i, e: (0, 0))],
        out_specs=pl.BlockSpec((tm, D), lambda i, e: (i, 0)),
        out_shape=jax.ShapeDtypeStruct((N, D), f32),
        scratch_shapes=[pltpu.VMEM((tm, D), bf16), pltpu.VMEM((tm, D), f32)],
        compiler_params=_cparams(("parallel", "arbitrary")), name="moe_ln2",
    )(x1, gates, wg, wu, wd, g, b)


def _sscore_kernel(pt_ref, qi_ref, wi_ref, kn_ref, *rest, pp):
    pages, (o_ref, on_ref) = rest[:pp], rest[pp:]
    qi = qi_ref[0]
    wi = wi_ref[0]
    tn = qi.shape[0] // IDX_HEADS

    def scores(keys):
        s = lax.dot_general(qi, keys, (((1,), (1,)), ((), ())), preferred_element_type=f32)
        r = jnp.maximum(s, 0.0) * wi
        return jnp.sum(r.reshape(tn, IDX_HEADS, keys.shape[0]), axis=1)

    for j in range(pp):
        o_ref[0, :, j * LANES:(j + 1) * LANES] = scores(pages[j][0].astype(bf16))

    @pl.when(pl.program_id(1) == 0)
    def _():
        on_ref[0] = scores(kn_ref[0])


def _sample_scores(page_table, qi, wi, ki_new, cache_kidx, pp):
    Bd, Tn = qi.shape[:2]
    n_pages = page_table.shape[1]
    ps = cache_kidx.shape[1]
    rows = Tn * IDX_HEADS
    qi2 = qi.reshape(Bd, rows, IDX_DIM).astype(bf16)
    wi2 = wi.reshape(Bd, rows, 1)
    kn = jnp.pad(ki_new, ((0, 0), (0, LANES - Tn), (0, 0))).astype(bf16)
    page_spec = lambda j: pl.BlockSpec((1, ps, IDX_DIM), lambda b, p, pt: (pt[b, p * pp + j], 0, 0))
    grid_spec = pltpu.PrefetchScalarGridSpec(
        num_scalar_prefetch=1, grid=(Bd, n_pages // pp),
        in_specs=[pl.BlockSpec((1, rows, IDX_DIM), lambda b, p, pt: (b, 0, 0)),
                  pl.BlockSpec((1, rows, 1), lambda b, p, pt: (b, 0, 0)),
                  pl.BlockSpec((1, LANES, IDX_DIM), lambda b, p, pt: (b, 0, 0))]
                 + [page_spec(j) for j in range(pp)],
        out_specs=[pl.BlockSpec((1, Tn, pp * ps), lambda b, p, pt: (b, 0, p)),
                   pl.BlockSpec((1, Tn, LANES), lambda b, p, pt: (b, 0, 0))])
    return pl.pallas_call(
        functools.partial(_sscore_kernel, pp=pp), grid_spec=grid_spec,
        out_shape=[jax.ShapeDtypeStruct((Bd, Tn, n_pages * ps), f32),
                   jax.ShapeDtypeStruct((Bd, Tn, LANES), f32)],
        compiler_params=_cparams(("parallel", "arbitrary")), name="sample_scores",
    )(page_table, qi2, wi2, kn, *([cache_kidx] * pp))


def _sselect_kernel(sp_ref, sn_ref, o_ref, keys_sc, *, k_top, tn, idx_bits):
    rows = sp_ref.shape[0]
    past = sp_ref.shape[1]
    n_tiles = past // LANES

    def load(j, _):
        c0 = pl.multiple_of(j * LANES, LANES)
        keys_sc[pl.ds(c0, LANES), :] = _order_key(jnp.transpose(sp_ref[:, pl.ds(c0, LANES)]))
        return 0

    lax.fori_loop(0, n_tiles, load, 0)
    new = _order_key(jnp.transpose(sn_ref[...]))
    jj = lax.broadcasted_iota(i32, new.shape, 0)
    qq = lax.broadcasted_iota(i32, new.shape, 1) % tn
    keys_sc[pl.ds(past, LANES), :] = jnp.where(jj <= qq, new, INT_MIN)
    thr = _select_rows(keys_sc, past + LANES, k_top, idx_bits, LANES)

    none = np.int32(2 ** 30)
    o_ref[...] = jnp.zeros(o_ref.shape, i32)
    slot = lax.broadcasted_iota(i32, o_ref.shape, 0)

    def slab(j, filled):
        c0 = pl.multiple_of(j * LANES, LANES)
        sel = keys_sc[pl.ds(c0, LANES), :] >= thr
        cand = jnp.where(sel, c0 + lax.broadcasted_iota(i32, sel.shape, 0), none)
        n_sel = jnp.sum(jnp.where(sel, 1, 0).astype(i32), axis=0, keepdims=True)

        def extract(i, cand):
            first = jnp.min(cand, axis=0, keepdims=True)
            o_ref[...] = jnp.where((slot == filled + i) & (first < none), first, o_ref[...])
            return jnp.where(cand == first, none, cand)

        lax.fori_loop(0, jnp.max(n_sel), extract, cand)
        return filled + n_sel

    lax.fori_loop(0, n_tiles + 1, slab, jnp.zeros((1, rows), i32))


def _sample_select(scores_past, scores_new, tn):
    assert scores_past.shape[0] <= LANES
    pad_rows = lambda a: jnp.pad(a, ((0, LANES - a.shape[0]), (0, 0)))
    scores_past, scores_new = pad_rows(scores_past), pad_rows(scores_new)
    R, past = scores_past.shape
    k_top = min(TOPK_MAX, (past + tn) // 4)
    assert past + 1 >= k_top
    kern = functools.partial(_sselect_kernel, k_top=k_top, tn=tn, idx_bits=(past + LANES - 1).bit_length())
    return pl.pallas_call(
        kern, grid=(1,),
        in_specs=[pl.BlockSpec((R, past), lambda i: (0, 0)), pl.BlockSpec((R, LANES), lambda i: (0, 0))],
        out_specs=pl.BlockSpec((k_top, R), lambda i: (0, 0)),
        out_shape=jax.ShapeDtypeStruct((k_top, R), i32),
        scratch_shapes=[pltpu.VMEM((past + LANES, R), i32)],
        compiler_params=_cparams(("arbitrary",)), name="sample_select",
    )(scores_past, scores_new)


def _sgather_kernel(idx_ref, row_ref, qT_ref, idxv_ref, knew_ref, vnew_ref, rb_ref, ck_hbm, cv_hbm, o_ref,
                    kbuf, vbuf, ksem, vsem, *, K, tn, past):
    r = pl.program_id(0)
    n = pl.num_programs(0)
    slot = r % 2

    def row_copies(rr, sl, j):
        row = row_ref[rr * K + j]
        return (pltpu.make_async_copy(ck_hbm.at[row], kbuf.at[sl, j], ksem.at[sl]),
                pltpu.make_async_copy(cv_hbm.at[row], vbuf.at[sl, j], vsem.at[sl]))

    def issue(rr, sl):
        def body(j, _):
            ck, cv = row_copies(rr, sl, j)
            ck.start(priority=0)
            cv.start(priority=1)
            return 0
        lax.fori_loop(0, K, body, 0, unroll=DMA_UNROLL)

    @pl.when(r == 0)
    def _():
        issue(0, 0)

    @pl.when(r + 1 < n)
    def _():
        issue(r + 1, 1 - slot)

    def wait_body(j, _):
        ck, cv = row_copies(r, slot, j)
        ck.wait()
        cv.wait()
        return 0

    lax.fori_loop(0, K, wait_body, 0, unroll=DMA_UNROLL)

    for t in range(tn):
        j = K - 1 - t
        pos = idx_ref[r * K + j]

        @pl.when(pos >= past)
        def _():
            kbuf[slot, j] = knew_ref[0, pos - past]
            vbuf[slot, j] = vnew_ref[0, pos - past]

    idxv = idxv_ref[0]
    k = kbuf[slot]
    v = vbuf[slot]
    pairs = jnp.dot(k.reshape(K * N_HEADS, HEAD_DIM).astype(bf16), qT_ref[0], preferred_element_type=f32)
    pairs = pairs.reshape(K, N_HEADS, LANES)
    diag = lax.broadcasted_iota(i32, pairs.shape, 1) == lax.broadcasted_iota(i32, pairs.shape, 2)
    logits = jnp.sum(jnp.where(diag, pairs, 0.0), axis=1)
    dist = past + r % tn - idxv
    onehot = jnp.where(_rel_bucket(dist) == lax.broadcasted_iota(i32, (K, LANES), 1), 1.0, 0.0)
    logits = logits + jnp.dot(onehot, rb_ref[...], preferred_element_type=f32, precision=lax.Precision.HIGHEST)
    p = jnp.exp(logits - jnp.max(logits, axis=0, keepdims=True))
    p = p / jnp.sum(p, axis=0, keepdims=True)
    pd = jnp.where(diag, jnp.broadcast_to(p[:, None, :], pairs.shape), 0.0).reshape(K * N_HEADS, LANES)
    spread = jnp.dot(pd.astype(bf16), jnp.ones((LANES, HEAD_DIM), bf16), preferred_element_type=f32)
    o_ref[0] = jnp.sum(spread.reshape(K, N_HEADS, HEAD_DIM) * v, axis=0)


def _sample_attention(page_table, q, k_new, v_new, idx_list, rel_bias, cache_k, cache_v):
    Bd, Tn = q.shape[:2]
    n_pages = page_table.shape[1]
    ps = cache_k.shape[1]
    past = n_pages * ps
    R = Bd * Tn
    K = idx_list.shape[0]
    idx = jnp.transpose(idx_list)[:R]
    past_idx = jnp.minimum(idx, past - 1).reshape(Bd, Tn * K)
    rows = jnp.take_along_axis(page_table, past_idx // ps, axis=1) * ps + past_idx % ps
    n_phys = cache_k.shape[0]
    qT = jnp.transpose((q * (HEAD_DIM ** -0.5)).reshape(R, N_HEADS, HEAD_DIM), (0, 2, 1))
    qT = jnp.pad(qT, ((0, 0), (0, 0), (0, LANES - N_HEADS))).astype(bf16)
    rb = jnp.pad(rel_bias.astype(f32), ((0, LANES - NUM_BUCKETS), (0, LANES - N_HEADS)))
    row = lambda r, idx_s, pt_s: (r, 0, 0)
    batch = lambda r, idx_s, pt_s: (r // Tn, 0, 0, 0)
    grid_spec = pltpu.PrefetchScalarGridSpec(
        num_scalar_prefetch=2, grid=(R,),
        in_specs=[pl.BlockSpec((1, HEAD_DIM, LANES), row),
                  pl.BlockSpec((1, K, 1), row),
                  pl.BlockSpec((1, Tn, N_HEADS, HEAD_DIM), batch),
                  pl.BlockSpec((1, Tn, N_HEADS, HEAD_DIM), batch),
                  pl.BlockSpec((LANES, LANES), lambda r, idx_s, pt_s: (0, 0)),
                  pl.BlockSpec(memory_space=pl.ANY),
                  pl.BlockSpec(memory_space=pl.ANY)],
        out_specs=pl.BlockSpec((1, N_HEADS, HEAD_DIM), row),
        scratch_shapes=[pltpu.VMEM((2, K, N_HEADS, HEAD_DIM), f32), pltpu.VMEM((2, K, N_HEADS, HEAD_DIM), f32),
                        pltpu.SemaphoreType.DMA((2,)), pltpu.SemaphoreType.DMA((2,))])
    kern = functools.partial(_sgather_kernel, K=K, tn=Tn, past=past)
    flat_cache = lambda c: c.reshape(n_phys * ps, N_HEADS, HEAD_DIM)
    out = pl.pallas_call(
        kern, grid_spec=grid_spec,
        out_shape=jax.ShapeDtypeStruct((R, N_HEADS, HEAD_DIM), f32),
        compiler_params=pltpu.CompilerParams(dimension_semantics=("arbitrary",), vmem_limit_bytes=VMEM_LIMIT,
                                             disable_bounds_checks=True),
        name="sample_attn",
    )(idx.reshape(R * K), rows.reshape(R * K).astype(i32), qT, idx.reshape(R, K, 1), k_new, v_new, rb,
      flat_cache(cache_k), flat_cache(cache_v))
    return out.reshape(Bd, Tn, N_HEADS * HEAD_DIM)


def _pick(n, cands):
    for c in cands:
        if n % c == 0:
            return c
    return n


def kernel(x_prompt, x_sample, cache_k, cache_v, cache_kidx, state_ssm_re, state_ssm_im, page_table, rel_bias, w_in, ssm_lambda_re, ssm_lambda_im, ssm_log_step, ssm_b_re, ssm_b_im, ssm_c_re, ssm_c_im, ssm_d, w_glu, w_attn_out, w_ssm_out, w_o, ln1_g, ln1_b, w_group_router, b_group_router, w_expert_router, b_expert_router, w_exp_gate, w_exp_up, w_exp_down, ln2_g, ln2_b):
    B, T, D = x_prompt.shape
    Bd, Tn = x_sample.shape[:2]
    n_phys, ps = cache_k.shape[1:3]
    d_attn = N_HEADS * HEAD_DIM
    d_qi = IDX_HEADS * IDX_DIM
    d_ssm = D // 2
    G = d_ssm // SSM_GROUP
    E = N_EXPERT_GROUPS * EXPERTS_PER_GROUP
    l = 0
    w = w_in[l]
    o_q, o_k, o_v, o_qi = 0, d_attn, 2 * d_attn, 3 * d_attn
    o_ki = o_qi + d_qi
    o_wi = o_ki + IDX_DIM
    o_u = o_wi + IDX_HEADS
    o_ga = o_u + d_ssm
    o_gb = o_ga + D
    cols = lambda a, n: w[:, a:a + n]

    small = jnp.concatenate([cols(o_ki, IDX_DIM), cols(o_wi, IDX_HEADS) * INDEXER_SCALE,
                             jnp.zeros((D, LANES - IDX_DIM - IDX_HEADS), f32)], axis=1)
    w_n = jnp.concatenate([cols(o_ga, D), cols(o_gb, D), cols(o_k, d_attn), cols(o_v, d_attn), small, cols(o_u, d_ssm)],
                          axis=1)
    c_ga, c_gb, c_k = 0, D, 2 * D
    c_v = c_k + d_attn
    c_s = c_v + d_attn
    c_u = c_s + LANES
    n_cols = c_u + d_ssm
    tn_n = 768
    w_n = jnp.pad(w_n, ((0, 0), (0, _round_up(n_cols, tn_n) - n_cols))).astype(bf16)
    w_t = jnp.concatenate([cols(o_q, d_attn) * (HEAD_DIM ** -0.5), cols(o_v, d_attn), cols(o_qi, d_qi)], axis=1).T.astype(bf16)
    w_wi = jnp.pad((cols(o_wi, IDX_HEADS) * INDEXER_SCALE).T, ((0, 2 * SUBLANES - IDX_HEADS), (0, 0))).astype(bf16)

    wa, wg, ws, wo = (a[l].astype(bf16) for a in (w_attn_out, w_glu, w_ssm_out, w_o))
    w_r = jnp.concatenate([w_expert_router[l], w_group_router[l], jnp.zeros((D, LANES - E - N_EXPERT_GROUPS), f32)], axis=1)
    b_r = jnp.concatenate([b_expert_router[l], b_group_router[l], jnp.zeros((LANES - E - N_EXPERT_GROUPS,), f32)])[None, :]
    weg, weu, wed = (a[l].astype(bf16) for a in (w_exp_gate, w_exp_up, w_exp_down))
    g1, b1, g2, b2 = (a[l][None, :].astype(f32) for a in (ln1_g, ln1_b, ln2_g, ln2_b))
    ssm_w = (ssm_lambda_re[l], ssm_lambda_im[l], ssm_log_step[l], ssm_b_re[l], ssm_b_im[l], ssm_c_re[l], ssm_c_im[l], ssm_d[l])

    def tail(x2, attn, ssm_y, zn):
        n = x2.shape[0]
        x1, x1_rows = _merge(x2, attn, ssm_y, zn, c_ga // D, c_gb // D, wa, wg, ws, wo, g1, b1, _pick(n, (256, 128)))
        gates, route, counts = _router(x1, w_r, b_r, _pick(n, (512, 128)))
        if 2 * n >= E * MOE_TILE:
            return _moe_routed(x1, x1_rows, route, counts, weg, weu, wed, g2, b2, MOE_TILE, _pick(n, (256, 128)))
        return _moe(x1, gates, weg, weu, wed, g2, b2, _pick(n, (512, 128)))

    N = B * T
    xp2 = x_prompt.reshape(N, D)
    xpb = xp2.astype(bf16)
    tm = _pick(N, (1024, 512, 256, 128))
    zn = _mm_nn(xpb, w_n, tm, tn_n)
    zt = _mm_nt(w_t, xpb, 512, tm, bf16)
    wit = _mm_nt(w_wi, xpb, 2 * SUBLANES, tm, f32)
    k_p, v_p, ki_p = zn[:, c_k:c_k + d_attn], zn[:, c_v:c_v + d_attn], zn[:, c_s:c_s + IDX_DIM]
    tq = _pick(T, (256, 128))
    attn_p = _prompt_attention(zt[:d_attn], zt[d_attn:2 * d_attn], zt[2 * d_attn:], wit,
                               ki_p.reshape(B, T, IDX_DIM).astype(bf16), k_p.astype(bf16),
                               _prompt_bias_tiles(rel_bias, tq), B, T, tq)
    L = _pick(T, (16, 8, 4, 2))
    zeros_state = jnp.zeros((B, G, STATE_DIM), f32)
    y_p, h_p = _ssm(zn, c_u // LANES, d_ssm, B, _pack_state(zeros_state, zeros_state)[:, :, None, :],
                    _ssm_tables(*ssm_w, L), L, True)
    hr_p, hi_p = _unpack_state(h_p[:, :, 0, :], G, STATE_DIM)
    out_p = tail(xp2, attn_p, y_p, zn).reshape(B, T, D)

    Ns = Bd * Tn
    xs2 = x_sample.reshape(Ns, D)
    xsb = xs2.astype(bf16)
    zs = _mm_nn(xsb, w_n, Ns, tn_n)
    w_s = jnp.concatenate([cols(o_q, d_attn), cols(o_qi, d_qi)], axis=1)
    w_s = jnp.pad(w_s, ((0, 0), (0, _round_up(d_attn + d_qi, tn_n) - d_attn - d_qi))).astype(bf16)
    zq = _mm_nn(xsb, w_s, Ns, tn_n)
    q_s = zq[:, :d_attn].reshape(Bd, Tn, N_HEADS, HEAD_DIM)
    qi_s = zq[:, d_attn:d_attn + d_qi].reshape(Bd, Tn, IDX_HEADS, IDX_DIM)
    k_s = zs[:, c_k:c_k + d_attn].reshape(Bd, Tn, N_HEADS, HEAD_DIM)
    v_s = zs[:, c_v:c_v + d_attn].reshape(Bd, Tn, N_HEADS, HEAD_DIM)
    ki_s = zs[:, c_s:c_s + IDX_DIM].reshape(Bd, Tn, IDX_DIM)
    wi_s = zs[:, c_s + IDX_DIM:c_s + IDX_DIM + IDX_HEADS].reshape(Bd, Tn, IDX_HEADS)
    n_pages = page_table.shape[1]
    pp = _pick(n_pages, (16, 8, 4, 2, 1))
    sc_past, sc_new = _sample_scores(page_table, qi_s, wi_s, ki_s, cache_kidx[l], pp)
    idx_list = _sample_select(sc_past.reshape(Ns, n_pages * ps), sc_new.reshape(Ns, LANES), Tn)
    attn_s = _sample_attention(page_table, q_s, k_s, v_s, idx_list, rel_bias, cache_k[l], cache_v[l])
    y_s, h_s = _ssm(zs, c_u // LANES, d_ssm, 1, _pack_state(state_ssm_re[l], state_ssm_im[l])[:, None],
                    _ssm_tables(*ssm_w, Tn), Tn, False)
    hr_s, hi_s = _unpack_state(h_s[:, 0], G, STATE_DIM)
    out_s = tail(xs2, attn_s.reshape(Ns, d_attn), y_s, zs).reshape(Bd, Tn, D)

    sdt = state_ssm_re.dtype
    return (out_p, out_s,
            k_p.reshape(1, B, T // ps, ps, N_HEADS, HEAD_DIM), v_p.reshape(1, B, T // ps, ps, N_HEADS, HEAD_DIM),
            ki_p.reshape(1, B, T // ps, ps, IDX_DIM), hr_p.astype(sdt)[None], hi_p.astype(sdt)[None],
            k_s[None], v_s[None], ki_s[None], hr_s.astype(sdt)[None], hi_s.astype(sdt)[None])
```
